```python
import jax, jax.numpy as jnp
from jax import lax
import numpy as np

D_MODEL = 4096
BATCH = 1
SEQ = 16384
DEPTH = 2

PLE_DIM = 256
D_MIX = D_MODEL
GLA_WIDTH = D_MIX // 2
GLA_HEADS = 4
GLA_DV = GLA_WIDTH // GLA_HEADS
GLA_DK = GLA_DV // 2
GLA_KEY = GLA_HEADS * GLA_DK
GLA_GATE_RANK = 16
GLA_GATE_NORM = 16.0
GDN_WIDTH = D_MIX - GLA_WIDTH
GDN_HEADS = 16
GDN_DK = 128
GDN_DV = GDN_WIDTH // GDN_HEADS
GDN_KEY = GDN_HEADS * GDN_DK
GDN_QKV = 2 * GDN_KEY + GDN_WIDTH
SHORT_CONV = 3
D_FF = 4 * D_MODEL
FFN_CONV = 3
CHUNK = 64
EPS = 1e-6
IN_SPLITS = (GLA_KEY, GLA_KEY, GLA_WIDTH, GLA_WIDTH, GLA_GATE_RANK, GLA_GATE_RANK,
             GDN_QKV, GDN_WIDTH, GDN_HEADS, GDN_HEADS, GDN_HEADS, GDN_HEADS)
D_IN = sum(IN_SPLITS)

kernel_name = "bidir_hybrid_gla_gdn_convffn_ple"


def rms_norm(x, gain):
    xf = x.astype(jnp.float32)
    y = xf * lax.rsqrt(jnp.mean(xf * xf, axis=-1, keepdims=True) + EPS)
    return (y * gain.astype(jnp.float32)).astype(x.dtype)


def unit_rms(x):
    xf = x.astype(jnp.float32)
    return (xf * lax.rsqrt(jnp.mean(xf * xf, axis=-1, keepdims=True) + EPS)).astype(x.dtype)


def l2_norm(x):
    return x * lax.rsqrt(jnp.sum(x * x, axis=-1, keepdims=True) + EPS)


def centred_dwconv(x, w):
    K, C = w.shape
    pad = K // 2
    return lax.conv_general_dilated(x, w[:, None, :].astype(x.dtype), window_strides=(1,),
                                    padding=[(pad, pad)],
                                    dimension_numbers=('NWC', 'WIO', 'NWC'),
                                    feature_group_count=C)


def to_heads(t, n_heads):
    B, S, _ = t.shape
    return t.reshape(B, S, n_heads, -1).transpose(0, 2, 1, 3).astype(jnp.float32)


def gla_chunked(q, k, v, g):
    B, H, S, dk = q.shape
    dv = v.shape[-1]
    nc = S // CHUNK
    chunks = lambda t: jnp.moveaxis(t.reshape(B, H, nc, CHUNK, t.shape[-1]), 2, 0)
    incl = jnp.tril(jnp.ones((CHUNK, CHUNK), bool))[:, :, None]

    def step(state, xs):
        qi, ki, vi, gi = xs
        b = jnp.cumsum(gi, axis=-2)
        diff = b[..., :, None, :] - b[..., None, :, :]
        decay = jnp.where(incl, jnp.exp(jnp.where(incl, diff, 0.0)), 0.0)
        attn = jnp.einsum('bhid,bhjd,bhijd->bhij', qi, ki, decay)
        o = (jnp.einsum('bhij,bhje->bhie', attn, vi)
             + jnp.einsum('bhid,bhde->bhie', qi * jnp.exp(b), state))
        b_last = b[..., -1:, :]
        state = (jnp.exp(b_last[..., 0, :])[..., None] * state
                 + jnp.einsum('bhjd,bhje->bhde', ki * jnp.exp(b_last - b), vi))
        return state, o

    s0 = jnp.zeros((B, H, dk, dv), jnp.float32)
    _, o = lax.scan(step, s0, (chunks(q), chunks(k), chunks(v), chunks(g)))
    return jnp.moveaxis(o, 0, 2).reshape(B, H, S, dv)


def gated_delta_chunked(q, k, v, g, beta):
    B, H, S, dk = q.shape
    dv = v.shape[-1]
    nc = S // CHUNK
    rs = lambda t: t.reshape(B, H, nc, CHUNK, *t.shape[3:])
    q, k, v, g, beta = rs(q), rs(k), rs(v), rs(g), rs(beta)
    gcum = jnp.cumsum(g, axis=-1)
    incl = jnp.tril(jnp.ones((CHUNK, CHUNK), bool))
    strict = jnp.tril(jnp.ones((CHUNK, CHUNK), bool), -1)
    diff = gcum[..., :, None] - gcum[..., None, :]
    gamma = jnp.where(incl, jnp.exp(jnp.where(incl, diff, 0.0)), 0.0)
    kb = k * beta[..., None]
    a_strict = jnp.where(strict, jnp.einsum('bhnid,bhnjd->bhnij', kb, k) * gamma, 0.0)
    eye = jnp.eye(CHUNK, dtype=jnp.float32)
    t_inv = lax.linalg.triangular_solve(eye + a_strict, jnp.broadcast_to(eye, a_strict.shape),
                                        left_side=True, lower=True, unit_diagonal=True)
    u = jnp.einsum('bhnij,bhnje->bhnie', t_inv, v * beta[..., None])
    w = jnp.einsum('bhnij,bhnjd->bhnid', t_inv, kb * jnp.exp(gcum)[..., None])
    qk = jnp.einsum('bhnid,bhnjd->bhnij', q, k) * gamma
    mv = lambda t: jnp.moveaxis(t, 2, 0)

    def step(state, xs):
        qi, ki, ui, wi, qki, gi = xs
        v_new = ui - jnp.einsum('bhid,bhde->bhie', wi, state)
        o = (jnp.einsum('bhid,bhde->bhie', qi * jnp.exp(gi)[..., None], state)
             + jnp.einsum('bhij,bhje->bhie', qki, v_new))
        g_last = gi[..., -1]
        state = (jnp.exp(g_last)[..., None, None] * state
                 + jnp.einsum('bhjd,bhje->bhde',
                              ki * jnp.exp(g_last[..., None] - gi)[..., None], v_new))
        return state, o

    s0 = jnp.zeros((B, H, dk, dv), jnp.float32)
    _, o = lax.scan(step, s0, (mv(q), mv(k), mv(u), mv(w), mv(qk), mv(gcum)))
    return jnp.moveaxis(o, 0, 2).reshape(B, H, S, dv)


def bidirectional(fn, fwd_args, bwd_args):
    flip = lambda t: jnp.flip(t, axis=2)
    return fn(*fwd_args) + flip(fn(*[flip(t) for t in bwd_args]))


def hybrid_mixer(h, w_in, gla_gate_w_f, gla_gate_b_f, gla_gate_w_b, gla_gate_b_b, gla_out_norm,
                 gdn_conv, gdn_a_log_f, gdn_dt_bias_f, gdn_a_log_b, gdn_dt_bias_b, gdn_out_norm,
                 w_out):
    B, S, _ = h.shape
    proj = h @ w_in
    idx = np.cumsum(IN_SPLITS)[:-1].tolist()
    (a_q, a_k, a_v, a_g, a_lr_f, a_lr_b,
     b_qkv, b_z, b_beta_f, b_beta_b, b_a_f, b_a_b) = jnp.split(proj, idx, axis=-1)

    q = to_heads(a_q, GLA_HEADS) * GLA_DK ** -0.5
    k = to_heads(a_k, GLA_HEADS)
    v = to_heads(a_v, GLA_HEADS)
    log_gate = lambda lr, w2, b2: to_heads(
        jax.nn.log_sigmoid((lr @ w2 + b2).astype(jnp.float32)) / GLA_GATE_NORM, GLA_HEADS)
    g_f = log_gate(a_lr_f, gla_gate_w_f, gla_gate_b_f)
    g_b = log_gate(a_lr_b, gla_gate_w_b, gla_gate_b_b)
    o_a = bidirectional(gla_chunked, (q, k, v, g_f), (q, k, v, g_b))
    o_a = o_a.transpose(0, 2, 1, 3)
    o_a = rms_norm(o_a, gla_out_norm) * jax.nn.silu(
        a_g.reshape(B, S, GLA_HEADS, GLA_DV).astype(jnp.float32))
    o_a = o_a.reshape(B, S, GLA_WIDTH)

    qkv = jax.nn.silu(centred_dwconv(b_qkv, gdn_conv))
    d_q, d_k, d_v = jnp.split(qkv, [GDN_KEY, 2 * GDN_KEY], axis=-1)
    q = l2_norm(to_heads(d_q, GDN_HEADS)) * GDN_DK ** -0.5
    k = l2_norm(to_heads(d_k, GDN_HEADS))
    v = to_heads(d_v, GDN_HEADS)
    beta = lambda bb: jax.nn.sigmoid(bb.astype(jnp.float32)).transpose(0, 2, 1)
    decay = lambda aa, a_log, dt_bias: (-jnp.exp(a_log.astype(jnp.float32)) * jax.nn.softplus(
        aa.astype(jnp.float32) + dt_bias.astype(jnp.float32))).transpose(0, 2, 1)
    o_b = bidirectional(gated_delta_chunked,
                        (q, k, v, decay(b_a_f, gdn_a_log_f, gdn_dt_bias_f), beta(b_beta_f)),
                        (q, k, v, decay(b_a_b, gdn_a_log_b, gdn_dt_bias_b), beta(b_beta_b)))
    o_b = o_b.transpose(0, 2, 1, 3)
    o_b = rms_norm(o_b, gdn_out_norm) * jax.nn.silu(
        b_z.reshape(B, S, GDN_HEADS, GDN_DV).astype(jnp.float32))
    o_b = o_b.reshape(B, S, GDN_WIDTH)

    o = jnp.concatenate([o_a, o_b], axis=-1).astype(h.dtype)
    return o @ w_out


def conv_ffn(h, w_up, conv_w, conv_b, w_down):
    u = centred_dwconv(h @ w_up, conv_w) + conv_b
    gate, up = jnp.split(u, 2, axis=-1)
    return (jax.nn.gelu(gate, approximate=True) * up) @ w_down


def setup_inputs(seed: int = 0) -> dict:
    key = jax.random.key(seed)
    ks = jax.random.split(key, 32)
    L = DEPTH
    nrm = lambda k, shape, fan_in: jax.random.normal(k, shape, jnp.float32) * fan_in ** -0.5
    gain = lambda k, n: 1.0 + 0.05 * jax.random.normal(k, (L, n), jnp.float32)
    small = lambda k, shape: 0.01 * jax.random.normal(k, shape, jnp.float32)
    a_log = lambda k: jnp.log(jax.random.uniform(k, (L, GDN_HEADS), jnp.float32, 1.0, 16.0))

    def dt_bias(k):
        dt = jnp.exp(jax.random.uniform(k, (L, GDN_HEADS), jnp.float32,
                                        np.log(1e-3), np.log(1e-1)))
        return dt + jnp.log(-jnp.expm1(-dt))

    return {
        "x": jax.random.normal(ks[0], (BATCH, SEQ, D_MODEL), jnp.float32),
        "p": jax.random.normal(ks[1], (DEPTH, BATCH, SEQ, PLE_DIM), jnp.float32),
        "norm_mix_pre": gain(ks[2], D_MODEL),
        "w_in": nrm(ks[3], (L, D_MODEL, D_IN), D_MODEL),
        "gla_gate_w_f": nrm(ks[4], (L, GLA_GATE_RANK, GLA_KEY), GLA_GATE_RANK),
        "gla_gate_b_f": small(ks[5], (L, GLA_KEY)),
        "gla_gate_w_b": nrm(ks[6], (L, GLA_GATE_RANK, GLA_KEY), GLA_GATE_RANK),
        "gla_gate_b_b": small(ks[7], (L, GLA_KEY)),
        "gla_out_norm": gain(ks[8], GLA_DV),
        "gdn_conv": nrm(ks[9], (L, SHORT_CONV, GDN_QKV), SHORT_CONV),
        "gdn_a_log_f": a_log(ks[10]),
        "gdn_dt_bias_f": dt_bias(ks[11]),
        "gdn_a_log_b": a_log(ks[12]),
        "gdn_dt_bias_b": dt_bias(ks[13]),
        "gdn_out_norm": gain(ks[14], GDN_DV),
        "w_out": nrm(ks[15], (L, D_MIX, D_MODEL), D_MIX),
        "norm_mix_post": gain(ks[16], D_MODEL),
        "norm_ffn_pre": gain(ks[17], D_MODEL),
        "ffn_w_up": nrm(ks[18], (L, D_MODEL, 2 * D_FF), D_MODEL),
        "ffn_conv_w": nrm(ks[19], (L, FFN_CONV, 2 * D_FF), FFN_CONV),
        "ffn_conv_b": small(ks[20], (L, 2 * D_FF)),
        "ffn_w_down": nrm(ks[21], (L, D_FF, D_MODEL), D_FF),
        "norm_ffn_post": gain(ks[22], D_MODEL),
        "ple_w_proj": nrm(ks[23], (L, PLE_DIM, D_MODEL), PLE_DIM),
        "ple_w_gate": nrm(ks[24], (L, D_MODEL, D_MODEL), D_MODEL),
        "norm_ple_post": gain(ks[25], D_MODEL),
    }


def reference(x, p, norm_mix_pre, w_in, gla_gate_w_f, gla_gate_b_f, gla_gate_w_b, gla_gate_b_b,
              gla_out_norm, gdn_conv, gdn_a_log_f, gdn_dt_bias_f, gdn_a_log_b, gdn_dt_bias_b,
              gdn_out_norm, w_out, norm_mix_post, norm_ffn_pre, ffn_w_up, ffn_conv_w, ffn_conv_b,
              ffn_w_down, norm_ffn_post, ple_w_proj, ple_w_gate, norm_ple_post):
    for i in range(DEPTH):
        h = rms_norm(x, norm_mix_pre[i])
        mix = hybrid_mixer(h, w_in[i], gla_gate_w_f[i], gla_gate_b_f[i], gla_gate_w_b[i],
                           gla_gate_b_b[i], gla_out_norm[i], gdn_conv[i], gdn_a_log_f[i],
                           gdn_dt_bias_f[i], gdn_a_log_b[i], gdn_dt_bias_b[i], gdn_out_norm[i],
                           w_out[i])
        x = x + rms_norm(mix, norm_mix_post[i])
        h = rms_norm(x, norm_ffn_pre[i])
        ffn = conv_ffn(h, ffn_w_up[i], ffn_conv_w[i], ffn_conv_b[i], ffn_w_down[i])
        x = x + rms_norm(ffn, norm_ffn_post[i])
        gate = jax.nn.sigmoid(unit_rms(x) @ ple_w_gate[i])
        x = x + rms_norm((p[i] @ ple_w_proj[i]) * gate, norm_ple_post[i])
    return x
```

```python
import functools

import numpy as np
import jax
import jax.numpy as jnp
from jax import lax
from jax.experimental import pallas as pl
from jax.experimental.pallas import tpu as pltpu

F32 = jnp.float32
BF16 = jnp.bfloat16
HIGHEST = lax.Precision.HIGHEST

D_MODEL = 4096
PLE_DIM = 256
GLA_HEADS, GLA_DK, GLA_DV, GLA_RANK = 4, 256, 512, 16
GLA_KEY = GLA_HEADS * GLA_DK
GLA_WIDTH = GLA_HEADS * GLA_DV
GLA_GATE_NORM = 16.0
GDN_HEADS, GDN_DK, GDN_DV = 16, 128, 128
GDN_KEY = GDN_HEADS * GDN_DK
GDN_WIDTH = GDN_HEADS * GDN_DV
D_FF = 4 * D_MODEL
EPS = 1e-6
IN_SPLITS = (GLA_KEY, GLA_KEY, GLA_WIDTH, GLA_WIDTH, GLA_RANK, GLA_RANK,
             2 * GDN_KEY + GDN_WIDTH, GDN_WIDTH, GDN_HEADS, GDN_HEADS, GDN_HEADS, GDN_HEADS)

GLA_HEAD_COLS = 2 * GLA_DK + GLA_DV
GDN_HEAD_COLS = 2 * GDN_DK + GDN_DV
OFF_GLA = 0
OFF_AG = OFF_GLA + GLA_HEADS * GLA_HEAD_COLS
OFF_GDN = OFF_AG + GLA_WIDTH
OFF_BZ = OFF_GDN + GDN_HEADS * GDN_HEAD_COLS
OFF_GATES = OFF_BZ + GDN_WIDTH
OFF_LR = OFF_GATES + 128
D_PROJ = OFF_LR + 128
GATE_SLOTS = 8

LANES = 128
GDN_CHUNK = 128
GLA_SUB = 16
GLA_ROWS = 128
VMEM_LIMIT = 56 * 1024 * 1024


def _cparams(sem):
    return pltpu.CompilerParams(dimension_semantics=sem, vmem_limit_bytes=VMEM_LIMIT)


def _dot(a, b):
    return jnp.dot(a.astype(BF16), b.astype(BF16), preferred_element_type=F32)


def _dot_nt(a, b):
    return lax.dot_general(a.astype(BF16), b.astype(BF16), (((1,), (1,)), ((), ())),
                           preferred_element_type=F32)


def _dot_tn(a, b):
    return lax.dot_general(a.astype(BF16), b.astype(BF16), (((0,), (0,)), ((), ())),
                           preferred_element_type=F32)


def _split(a):
    hi = a.astype(BF16)
    lo = (a - hi.astype(F32)).astype(BF16)
    return hi, lo


def _dot3(a, b):
    ah, al = _split(a)
    bh, bl = _split(b)
    d = functools.partial(jnp.dot, preferred_element_type=F32)
    return d(ah, bh) + (d(ah, bl) + d(al, bh))


def _sigmoid(x):
    return 1.0 / (1.0 + jnp.exp(-x))


def _softplus(x):
    return jnp.maximum(x, 0.0) + jnp.log1p(jnp.exp(-jnp.abs(x)))


def _rms(x):
    return x * lax.rsqrt(jnp.mean(x * x, axis=-1, keepdims=True) + EPS)


def _centred_conv3(x, prev_row, next_row, w):
    rows = x.shape[0]
    rid = lax.broadcasted_iota(jnp.int32, x.shape, 0)
    x_prev = jnp.where(rid == 0, prev_row, pltpu.roll(x, 1, 0))
    x_next = jnp.where(rid == rows - 1, next_row, pltpu.roll(x, rows - 1, 0))
    return x_prev * w[0:1, :] + x * w[1:2, :] + x_next * w[2:3, :]


def _norm_kernel(x_ref, g_ref, o_ref):
    o_ref[...] = (_rms(x_ref[...]) * g_ref[...]).astype(o_ref.dtype)


def _norm(x, gain, tm=256):
    m, d = x.shape
    return pl.pallas_call(
        _norm_kernel,
        grid=(m // tm,),
        in_specs=[pl.BlockSpec((tm, d), lambda i: (i, 0)), pl.BlockSpec((1, d), lambda i: (0, 0))],
        out_specs=pl.BlockSpec((tm, d), lambda i: (i, 0)),
        out_shape=jax.ShapeDtypeStruct((m, d), BF16),
        compiler_params=_cparams(("parallel",)),
        name="rms_norm",
    )(x, gain.reshape(1, d))


def _post_kernel(x_ref, y_ref, gp_ref, gn_ref, xo_ref, ho_ref):
    xn = x_ref[...] + _rms(y_ref[...]) * gp_ref[...]
    xo_ref[...] = xn
    ho_ref[...] = (_rms(xn) * gn_ref[...]).astype(ho_ref.dtype)


def _post_last_kernel(x_ref, y_ref, gp_ref, xo_ref):
    xo_ref[...] = x_ref[...] + _rms(y_ref[...]) * gp_ref[...]


def _post(x, y, g_post, g_next, tm=256):
    m, d = x.shape
    row = pl.BlockSpec((tm, d), lambda i: (i, 0))
    vec = pl.BlockSpec((1, d), lambda i: (0, 0))
    if g_next is None:
        return pl.pallas_call(
            _post_last_kernel, grid=(m // tm,), in_specs=[row, row, vec], out_specs=row,
            out_shape=jax.ShapeDtypeStruct((m, d), F32), compiler_params=_cparams(("parallel",)),
            name="post_last")(x, y, g_post.reshape(1, d)), None
    return pl.pallas_call(
        _post_kernel, grid=(m // tm,), in_specs=[row, row, vec, vec], out_specs=[row, row],
        out_shape=[jax.ShapeDtypeStruct((m, d), F32), jax.ShapeDtypeStruct((m, d), BF16)],
        compiler_params=_cparams(("parallel",)), name="post",
    )(x, y, g_post.reshape(1, d), g_next.reshape(1, d))


def _mm_kernel(a_ref, w_ref, o_ref):
    o_ref[...] = jnp.dot(a_ref[...], w_ref[...], preferred_element_type=F32).astype(o_ref.dtype)


def _mm(a, w, tm, tn, out_dtype=F32, name="mm"):
    m, k = a.shape
    n = w.shape[1]
    return pl.pallas_call(
        _mm_kernel,
        grid=(n // tn, m // tm),
        in_specs=[pl.BlockSpec((tm, k), lambda j, i: (i, 0)), pl.BlockSpec((k, tn), lambda j, i: (0, j))],
        out_specs=pl.BlockSpec((tm, tn), lambda j, i: (i, j)),
        out_shape=jax.ShapeDtypeStruct((m, n), out_dtype),
        compiler_params=_cparams(("parallel", "parallel")),
        name=name,
    )(a, w)


def _mm_ksplit_kernel(a_ref, w_ref, o_ref, acc_ref, *, nk):
    kk = pl.program_id(2)

    @pl.when(kk == 0)
    def _():
        acc_ref[...] = jnp.zeros_like(acc_ref)

    acc_ref[...] += jnp.dot(a_ref[...], w_ref[...], preferred_element_type=F32)

    @pl.when(kk == nk - 1)
    def _():
        o_ref[...] = acc_ref[...]


def _mm_ksplit(a, w, tm, tn, tk, name="mm_ksplit"):
    m, k = a.shape
    n = w.shape[1]
    nk = k // tk
    return pl.pallas_call(
        functools.partial(_mm_ksplit_kernel, nk=nk),
        grid=(m // tm, n // tn, nk),
        in_specs=[pl.BlockSpec((tm, tk), lambda i, j, kk: (i, kk)),
                  pl.BlockSpec((tk, tn), lambda i, j, kk: (kk, j))],
        out_specs=pl.BlockSpec((tm, tn), lambda i, j, kk: (i, j)),
        out_shape=jax.ShapeDtypeStruct((m, n), F32),
        scratch_shapes=[pltpu.VMEM((tm, tn), F32)],
        compiler_params=_cparams(("parallel", "parallel", "arbitrary")),
        name=name,
    )(a, w)


HALO = 16


def _ffn_up_kernel(h_ref, hp_ref, hn_ref, wg_ref, wu_ref, cwg_ref, cwu_ref, cbg_ref, cbu_ref, o_ref,
                   *, tm, nm):
    i = pl.program_id(1)
    lhs = jnp.concatenate([hp_ref[...], h_ref[...], hn_ref[...]], axis=0)
    not_first = (i > 0).astype(F32)
    not_last = (i < nm - 1).astype(F32)

    def branch(w_ref, cw_ref, cb_ref):
        y = jnp.dot(lhs, w_ref[...], preferred_element_type=F32)
        prev_row = y[HALO - 1:HALO, :] * not_first
        next_row = y[HALO + tm:HALO + tm + 1, :] * not_last
        return _centred_conv3(y[HALO:HALO + tm, :], prev_row, next_row, cw_ref[...]) + cb_ref[...]

    gate = branch(wg_ref, cwg_ref, cbg_ref)
    up = branch(wu_ref, cwu_ref, cbu_ref)
    c0 = np.float32(np.sqrt(2.0 / np.pi))
    gelu = 0.5 * gate * (1.0 + jnp.tanh(c0 * (gate + np.float32(0.044715) * (gate * gate * gate))))
    o_ref[...] = (gelu * up).astype(o_ref.dtype)


def _ffn_up(h, w_up, conv_w, conv_b, tm=512, tn=512):
    m, k = h.shape
    nm, nn = m // tm, D_FF // tn
    rb = tm // HALO
    nhb = m // HALO
    conv_b = conv_b.reshape(1, 2 * D_FF)
    return pl.pallas_call(
        functools.partial(_ffn_up_kernel, tm=tm, nm=nm),
        grid=(nn, nm),
        in_specs=[
            pl.BlockSpec((tm, k), lambda j, i: (i, 0)),
            pl.BlockSpec((HALO, k), lambda j, i: (jnp.maximum(i * rb - 1, 0), 0)),
            pl.BlockSpec((HALO, k), lambda j, i: (jnp.minimum((i + 1) * rb, nhb - 1), 0)),
            pl.BlockSpec((k, tn), lambda j, i: (0, j)),
            pl.BlockSpec((k, tn), lambda j, i: (0, j + nn)),
            pl.BlockSpec((3, tn), lambda j, i: (0, j)),
            pl.BlockSpec((3, tn), lambda j, i: (0, j + nn)),
            pl.BlockSpec((1, tn), lambda j, i: (0, j)),
            pl.BlockSpec((1, tn), lambda j, i: (0, j + nn)),
        ],
        out_specs=pl.BlockSpec((tm, tn), lambda j, i: (i, j)),
        out_shape=jax.ShapeDtypeStruct((m, D_FF), BF16),
        compiler_params=_cparams(("parallel", "parallel")),
        name="ffn_up",
    )(h, h, h, w_up, w_up, conv_w, conv_w, conv_b, conv_b)


def _ple_kernel(u_ref, wg_ref, p_ref, wp_ref, o_ref):
    gate = _sigmoid(jnp.dot(u_ref[...], wg_ref[...], preferred_element_type=F32))
    proj = jnp.dot(p_ref[...].astype(BF16), wp_ref[...], preferred_element_type=F32)
    o_ref[...] = proj * gate


def _ple(u, w_gate, p, w_proj, tm=512, tn=1024):
    m, k = u.shape
    n = w_gate.shape[1]
    return pl.pallas_call(
        _ple_kernel,
        grid=(n // tn, m // tm),
        in_specs=[pl.BlockSpec((tm, k), lambda j, i: (i, 0)), pl.BlockSpec((k, tn), lambda j, i: (0, j)),
                  pl.BlockSpec((tm, PLE_DIM), lambda j, i: (i, 0)),
                  pl.BlockSpec((PLE_DIM, tn), lambda j, i: (0, j))],
        out_specs=pl.BlockSpec((tm, tn), lambda j, i: (i, j)),
        out_shape=jax.ShapeDtypeStruct((m, n), F32),
        compiler_params=_cparams(("parallel", "parallel")),
        name="ple_gate",
    )(u, w_gate, p, w_proj)


def _gdn_gates_kernel(x_ref, alog_ref, dtb_ref, o_ref):
    x = x_ref[...]
    n = x.shape[0]
    slot = lax.broadcasted_iota(jnp.int32, x.shape, 1) % GATE_SLOTS
    decay = -jnp.exp(alog_ref[...]) * _softplus(x + dtb_ref[...])
    ri = lax.broadcasted_iota(jnp.int32, (n, n), 0)
    ci = lax.broadcasted_iota(jnp.int32, (n, n), 1)
    prefix = jnp.dot((ci <= ri).astype(F32), decay, precision=HIGHEST, preferred_element_type=F32)
    suffix = jnp.dot((ci >= ri).astype(F32), decay, precision=HIGHEST, preferred_element_type=F32)
    vals = jnp.where(slot < 2, _sigmoid(x), jnp.where(slot == 2, prefix, suffix))
    o_ref[...] = vals.T


def _gdn_gates(proj, alog_lanes, dtb_lanes):
    m = proj.shape[0]
    cb = OFF_GATES // LANES
    return pl.pallas_call(
        _gdn_gates_kernel,
        grid=(m // GDN_CHUNK,),
        in_specs=[pl.BlockSpec((GDN_CHUNK, LANES), lambda c: (c, cb)),
                  pl.BlockSpec((1, LANES), lambda c: (0, 0)),
                  pl.BlockSpec((1, LANES), lambda c: (0, 0))],
        out_specs=pl.BlockSpec((LANES, GDN_CHUNK), lambda c: (0, c)),
        out_shape=jax.ShapeDtypeStruct((LANES, m), F32),
        compiler_params=_cparams(("parallel",)),
        name="gdn_gates",
    )(proj, alog_lanes, dtb_lanes)


def _tri_inverse(a):
    c = a.shape[0]
    eye = (lax.broadcasted_iota(jnp.int32, (c, c), 0) == lax.broadcasted_iota(jnp.int32, (c, c), 1)).astype(F32)
    p = -a
    t = eye + p
    steps = int(np.log2(c)) - 1
    for s in range(steps):
        p = _dot3(p, p)
        t = t + _dot3(t, p)
    return t


def _gdn_chunk(q, k, v, beta_row, gc_row, s, rev):
    c = q.shape[0]
    ri = lax.broadcasted_iota(jnp.int32, (c, c), 0)
    ci = lax.broadcasted_iota(jnp.int32, (c, c), 1)
    incl = (ci >= ri) if rev else (ci <= ri)
    strict = (ci > ri) if rev else (ci < ri)
    g_row = jnp.broadcast_to(gc_row, (c, c))
    g_col = g_row.T
    beta_col = jnp.broadcast_to(beta_row, (c, c)).T
    gamma = jnp.where(incl, jnp.exp(jnp.where(incl, g_col - g_row, 0.0)), 0.0)
    kb = k * beta_col
    a = jnp.where(strict, _dot_nt(kb, k) * gamma, 0.0)
    t_inv = _tri_inverse(a)
    e_col = jnp.exp(g_col)
    u = _dot(t_inv, v * beta_col)
    w = _dot(t_inv, kb * e_col)
    qk = jnp.where(incl, _dot_nt(q, k) * gamma, 0.0)
    v_new = u - _dot(w, s)
    o = _dot(q * e_col, s) + _dot(qk, v_new)
    g_last = gc_row[:, 0:1] if rev else gc_row[:, c - 1:c]
    k_dec = k * jnp.exp(g_last - g_col)
    s_new = jnp.exp(g_last) * s + _dot_tn(k_dec, v_new)
    return o, s_new


def _gdn_kernel(xf_ref, xfp_ref, xfn_ref, xb_ref, xbp_ref, xbn_ref, cw_ref, gf_ref, gb_ref,
                of_ref, ob_ref, sf_ref, sb_ref, *, hp, nch):
    c = pl.program_id(1)

    @pl.when(c == 0)
    def _():
        sf_ref[...] = jnp.zeros_like(sf_ref)
        sb_ref[...] = jnp.zeros_like(sb_ref)

    cw = cw_ref[...]
    dirs = ((xf_ref, xfp_ref, xfn_ref, gf_ref, of_ref, sf_ref, False, c),
            (xb_ref, xbp_ref, xbn_ref, gb_ref, ob_ref, sb_ref, True, nch - 1 - c))
    for x_ref, xp_ref, xn_ref, g_ref, o_ref, s_ref, rev, chunk in dirs:
        prev_row = xp_ref[7:8, :] * (chunk > 0).astype(F32)
        next_row = xn_ref[0:1, :] * (chunk < nch - 1).astype(F32)
        y = _centred_conv3(x_ref[...], prev_row, next_row, cw)
        y = y * _sigmoid(y)
        for h in range(hp):
            b0 = h * GDN_HEAD_COLS
            q = y[:, b0:b0 + GDN_DK]
            k = y[:, b0 + GDN_DK:b0 + 2 * GDN_DK]
            v = y[:, b0 + 2 * GDN_DK:b0 + GDN_HEAD_COLS]
            q = q * lax.rsqrt(jnp.sum(q * q, axis=-1, keepdims=True) + EPS) * np.float32(GDN_DK ** -0.5)
            k = k * lax.rsqrt(jnp.sum(k * k, axis=-1, keepdims=True) + EPS)
            g0 = h * GATE_SLOTS
            beta_row = g_ref[g0 + 1:g0 + 2, :] if rev else g_ref[g0:g0 + 1, :]
            gc_row = g_ref[g0 + 3:g0 + 4, :] if rev else g_ref[g0 + 2:g0 + 3, :]
            o, s_new = _gdn_chunk(q, k, v, beta_row, gc_row, s_ref[h], rev)
            s_ref[h] = s_new
            o_ref[:, h * GDN_DV:(h + 1) * GDN_DV] = o


def _gdn(proj, conv_w_perm, gates_t, hp=2):
    m = proj.shape[0]
    nch = m // GDN_CHUNK
    width = hp * GDN_HEAD_COLS
    cb0 = OFF_GDN // width
    r8 = GDN_CHUNK // 8
    n8 = m // 8
    fwd = lambda g, c: c
    bwd = lambda g, c: nch - 1 - c

    def main(ch):
        return pl.BlockSpec((GDN_CHUNK, width), lambda g, c: (ch(g, c), cb0 + g))

    def prev8(ch):
        return pl.BlockSpec((8, width), lambda g, c: (jnp.maximum(ch(g, c) * r8 - 1, 0), cb0 + g))

    def next8(ch):
        return pl.BlockSpec((8, width), lambda g, c: (jnp.minimum((ch(g, c) + 1) * r8, n8 - 1), cb0 + g))

    def gates(ch):
        return pl.BlockSpec((hp * GATE_SLOTS, GDN_CHUNK), lambda g, c: (g, ch(g, c)))

    def out(ch):
        return pl.BlockSpec((GDN_CHUNK, hp * GDN_DV), lambda g, c: (ch(g, c), g))

    return pl.pallas_call(
        functools.partial(_gdn_kernel, hp=hp, nch=nch),
        grid=(GDN_HEADS // hp, nch),
        in_specs=[main(fwd), prev8(fwd), next8(fwd), main(bwd), prev8(bwd), next8(bwd),
                  pl.BlockSpec((3, width), lambda g, c: (0, g)), gates(fwd), gates(bwd)],
        out_specs=[out(fwd), out(bwd)],
        out_shape=[jax.ShapeDtypeStruct((m, GDN_WIDTH), F32)] * 2,
        scratch_shapes=[pltpu.VMEM((hp, GDN_DK, GDN_DV), F32)] * 2,
        compiler_params=_cparams(("parallel", "arbitrary")),
        name="gdn",
    )(proj, proj, proj, proj, proj, proj, conv_w_perm, gates_t, gates_t)


def _gla_dir(x, lr, w2, b2, st_ref, o_ref, rev):
    rows = x.shape[0]
    nsub = rows // GLA_SUB
    logit = jnp.dot(lr, w2, precision=HIGHEST, preferred_element_type=F32) + b2
    g = (jnp.minimum(logit, 0.0) - jnp.log1p(jnp.exp(-jnp.abs(logit)))) * np.float32(1.0 / GLA_GATE_NORM)
    q = x[:, 0:GLA_DK] * np.float32(GLA_DK ** -0.5)
    k = x[:, GLA_DK:2 * GLA_DK]
    v = x[:, 2 * GLA_DK:GLA_HEAD_COLS]

    ri = lax.broadcasted_iota(jnp.int32, (rows, rows), 0)
    ci = lax.broadcasted_iota(jnp.int32, (rows, rows), 1)
    same = (ri // GLA_SUB) == (ci // GLA_SUB)
    tri = same & ((ci >= ri) if rev else (ci <= ri))
    b = jnp.dot(tri.astype(F32), g, precision=HIGHEST, preferred_element_type=F32)
    b_tot = jnp.dot(same.astype(F32), g, precision=HIGHEST, preferred_element_type=F32)
    q_dec = q * jnp.exp(b)
    k_dec = k * jnp.exp(b_tot - b)

    b3 = b.reshape(nsub, GLA_SUB, GLA_DK)
    q3 = q.reshape(nsub, GLA_SUB, GLA_DK)
    k3 = k.reshape(nsub, GLA_SUB, GLA_DK)
    lane = lax.broadcasted_iota(jnp.int32, (nsub, GLA_SUB, GLA_SUB), 2)
    sub = lax.broadcasted_iota(jnp.int32, (nsub, GLA_SUB, GLA_SUB), 1)
    att = jnp.zeros((nsub, GLA_SUB, GLA_SUB), F32)
    for j in range(GLA_SUB):
        e = jnp.exp(jnp.minimum(b3 - b3[:, j:j + 1, :], 0.0))
        col = jnp.sum(q3 * e * k3[:, j:j + 1, :], axis=-1, keepdims=True)
        att = jnp.where(lane == j, col, att)
    att = jnp.where((lane >= sub) if rev else (lane <= sub), att, 0.0)

    order = range(nsub - 1, -1, -1) if rev else range(nsub)
    for s in order:
        r0 = s * GLA_SUB
        st = st_ref[...]
        v_s = v[r0:r0 + GLA_SUB, :]
        o = _dot_nt(q_dec[r0:r0 + GLA_SUB, :], st) + _dot(att[s], v_s)
        o_ref[r0:r0 + GLA_SUB, :] = o
        st_ref[...] = st * jnp.exp(b_tot[r0:r0 + 1, :]) + _dot_tn(v_s, k_dec[r0:r0 + GLA_SUB, :])


def _gla_kernel(xf_ref, xb_ref, lrf_ref, lrb_ref, w2f_ref, w2b_ref, b2f_ref, b2b_ref,
                of_ref, ob_ref, sf_ref, sb_ref):
    c = pl.program_id(1)

    @pl.when(c == 0)
    def _():
        sf_ref[...] = jnp.zeros_like(sf_ref)
        sb_ref[...] = jnp.zeros_like(sb_ref)

    _gla_dir(xf_ref[...], lrf_ref[...], w2f_ref[...], b2f_ref[...], sf_ref, of_ref, False)
    _gla_dir(xb_ref[...], lrb_ref[...], w2b_ref[...], b2b_ref[...], sb_ref, ob_ref, True)


def _gla(proj, w2f_pad, w2b_pad, b2f, b2b):
    m = proj.shape[0]
    nch = m // GLA_ROWS
    lr_cb = OFF_LR // LANES
    fwd = lambda h, c: c
    bwd = lambda h, c: nch - 1 - c

    def spec(cols, ch, col_block):
        return pl.BlockSpec((GLA_ROWS, cols), lambda h, c: (ch(h, c), col_block(h)))

    return pl.pallas_call(
        _gla_kernel,
        grid=(GLA_HEADS, nch),
        in_specs=[spec(GLA_HEAD_COLS, fwd, lambda h: h), spec(GLA_HEAD_COLS, bwd, lambda h: h),
                  spec(LANES, fwd, lambda h: lr_cb), spec(LANES, bwd, lambda h: lr_cb),
                  pl.BlockSpec((LANES, GLA_DK), lambda h, c: (0, h)),
                  pl.BlockSpec((LANES, GLA_DK), lambda h, c: (0, h)),
                  pl.BlockSpec((1, GLA_DK), lambda h, c: (0, h)),
                  pl.BlockSpec((1, GLA_DK), lambda h, c: (0, h))],
        out_specs=[spec(GLA_DV, fwd, lambda h: h), spec(GLA_DV, bwd, lambda h: h)],
        out_shape=[jax.ShapeDtypeStruct((m, GLA_WIDTH), F32)] * 2,
        scratch_shapes=[pltpu.VMEM((GLA_DV, GLA_DK), F32)] * 2,
        compiler_params=_cparams(("parallel", "arbitrary")),
        name="gla",
    )(proj, proj, proj, proj, w2f_pad, w2b_pad, b2f, b2b)


def _mix_out_kernel(af_ref, ab_ref, ag_ref, bf_ref, bb_ref, bz_ref, na_ref, nb_ref, o_ref):
    def heads(f_ref, b_ref, z_ref, gain_ref, n_heads, width, base):
        for h in range(n_heads):
            sl = slice(h * width, (h + 1) * width)
            o = _rms(f_ref[:, sl] + b_ref[:, sl]) * gain_ref[...]
            z = z_ref[:, sl]
            o_ref[:, base + h * width:base + (h + 1) * width] = (o * (z * _sigmoid(z))).astype(o_ref.dtype)

    heads(af_ref, ab_ref, ag_ref, na_ref, GLA_HEADS, GLA_DV, 0)
    heads(bf_ref, bb_ref, bz_ref, nb_ref, GDN_HEADS, GDN_DV, GLA_WIDTH)


def _mix_out(oaf, oab, obf, obb, proj, gla_norm, gdn_norm, tm=256):
    m = proj.shape[0]
    half = pl.BlockSpec((tm, GLA_WIDTH), lambda i: (i, 0))
    return pl.pallas_call(
        _mix_out_kernel,
        grid=(m // tm,),
        in_specs=[half, half, pl.BlockSpec((tm, GLA_WIDTH), lambda i: (i, OFF_AG // GLA_WIDTH)),
                  half, half, pl.BlockSpec((tm, GDN_WIDTH), lambda i: (i, OFF_BZ // GDN_WIDTH)),
                  pl.BlockSpec((1, GLA_DV), lambda i: (0, 0)), pl.BlockSpec((1, GDN_DV), lambda i: (0, 0))],
        out_specs=pl.BlockSpec((tm, D_MODEL), lambda i: (i, 0)),
        out_shape=jax.ShapeDtypeStruct((m, D_MODEL), BF16),
        compiler_params=_cparams(("parallel",)),
        name="mix_out",
    )(oaf, oab, proj, obf, obb, proj, gla_norm.reshape(1, GLA_DV), gdn_norm.reshape(1, GDN_DV))


def _permute_w_in(w):
    kdim = w.shape[0]
    idx = np.cumsum(IN_SPLITS)[:-1].tolist()
    (a_q, a_k, a_v, a_g, lr_f, lr_b, b_qkv, b_z, beta_f, beta_b, dec_f, dec_b) = jnp.split(w, idx, axis=1)
    gla = jnp.concatenate([a_q.reshape(kdim, GLA_HEADS, GLA_DK), a_k.reshape(kdim, GLA_HEADS, GLA_DK),
                           a_v.reshape(kdim, GLA_HEADS, GLA_DV)], axis=2).reshape(kdim, -1)
    d_q, d_k, d_v = jnp.split(b_qkv, [GDN_KEY, 2 * GDN_KEY], axis=1)
    gdn = jnp.stack([d_q.reshape(kdim, GDN_HEADS, GDN_DK), d_k.reshape(kdim, GDN_HEADS, GDN_DK),
                     d_v.reshape(kdim, GDN_HEADS, GDN_DV)], axis=2).reshape(kdim, -1)
    gates = jnp.stack([beta_f, beta_b, dec_f, dec_b], axis=2)
    gates = jnp.pad(gates, ((0, 0), (0, 0), (0, GATE_SLOTS - 4))).reshape(kdim, GDN_HEADS * GATE_SLOTS)
    lr = jnp.pad(jnp.concatenate([lr_f, lr_b], axis=1), ((0, 0), (0, LANES - 2 * GLA_RANK)))
    return jnp.concatenate([gla, a_g, gdn, b_z, gates, lr], axis=1).astype(BF16)


def _permute_gdn_conv(cw):
    d_q, d_k, d_v = jnp.split(cw, [GDN_KEY, 2 * GDN_KEY], axis=1)
    return jnp.stack([d_q.reshape(3, GDN_HEADS, GDN_DK), d_k.reshape(3, GDN_HEADS, GDN_DK),
                      d_v.reshape(3, GDN_HEADS, GDN_DV)], axis=2).reshape(3, -1)


def _gate_lanes(f, b):
    z = jnp.zeros_like(f)
    return jnp.stack([z, z, f, b, z, z, z, z], axis=1).reshape(1, GDN_HEADS * GATE_SLOTS)


def _pad_gate_w(w, first_row):
    return jnp.zeros((LANES, GLA_KEY), F32).at[first_row:first_row + GLA_RANK, :].set(w)


def kernel(x, p, norm_mix_pre, w_in, gla_gate_w_f, gla_gate_b_f, gla_gate_w_b, gla_gate_b_b, gla_out_norm, gdn_conv, gdn_a_log_f, gdn_dt_bias_f, gdn_a_log_b, gdn_dt_bias_b, gdn_out_norm, w_out, norm_mix_post, norm_ffn_pre, ffn_w_up, ffn_conv_w, ffn_conv_b, ffn_w_down, norm_ffn_post, ple_w_proj, ple_w_gate, norm_ple_post):
    bsz, seq, d = x.shape
    depth = w_in.shape[0]
    m = bsz * seq
    assert bsz == 1, "the scans treat the flattened rows as one sequence"
    xm = x.reshape(m, d)
    h = _norm(xm, norm_mix_pre[0])
    for i in range(depth):
        proj = _mm(h, _permute_w_in(w_in[i]), tm=512, tn=768, name="in_proj")
        gates_t = _gdn_gates(proj, _gate_lanes(gdn_a_log_f[i], gdn_a_log_b[i]),
                             _gate_lanes(gdn_dt_bias_f[i], gdn_dt_bias_b[i]))
        oaf, oab = _gla(proj, _pad_gate_w(gla_gate_w_f[i], 0), _pad_gate_w(gla_gate_w_b[i], GLA_RANK),
                        gla_gate_b_f[i].reshape(1, GLA_KEY), gla_gate_b_b[i].reshape(1, GLA_KEY))
        obf, obb = _gdn(proj, _permute_gdn_conv(gdn_conv[i]), gates_t)
        o = _mix_out(oaf, oab, obf, obb, proj, gla_out_norm[i], gdn_out_norm[i])
        mix = _mm(o, w_out[i].astype(BF16), tm=512, tn=1024, name="out_proj")
        xm, h = _post(xm, mix, norm_mix_post[i], norm_ffn_pre[i])
        act = _ffn_up(h, ffn_w_up[i].astype(BF16), ffn_conv_w[i], ffn_conv_b[i])
        ffn = _mm_ksplit(act, ffn_w_down[i].astype(BF16), tm=1024, tn=1024, tk=2048, name="ffn_down")
        xm, u = _post(xm, ffn, norm_ffn_post[i], jnp.ones((d,), F32))
        t = _ple(u, ple_w_gate[i].astype(BF16), p[i].reshape(m, PLE_DIM), ple_w_proj[i].astype(BF16))
        g_next = norm_mix_pre[i + 1] if i + 1 < depth else None
        xm, h = _post(xm, t, norm_ple_post[i], g_next)
    return xm.reshape(bsz, seq, d)
```

```python
import functools

import numpy as np
import jax
import jax.numpy as jnp
from jax import lax
from jax.experimental import pallas as pl
from jax.experimental.pallas import tpu as pltpu

F32 = jnp.float32
BF16 = jnp.bfloat16
HIGHEST = lax.Precision.HIGHEST

D_MODEL = 4096
PLE_DIM = 256
GLA_HEADS, GLA_DK, GLA_DV, GLA_RANK = 4, 256, 512, 16
GLA_KEY = GLA_HEADS * GLA_DK
GLA_WIDTH = GLA_HEADS * GLA_DV
GLA_GATE_NORM = 16.0
GDN_HEADS, GDN_DK, GDN_DV = 16, 128, 128
GDN_KEY = GDN_HEADS * GDN_DK
GDN_WIDTH = GDN_HEADS * GDN_DV
D_FF = 4 * D_MODEL
EPS = 1e-6
IN_SPLITS = (GLA_KEY, GLA_KEY, GLA_WIDTH, GLA_WIDTH, GLA_RANK, GLA_RANK,
             2 * GDN_KEY + GDN_WIDTH, GDN_WIDTH, GDN_HEADS, GDN_HEADS, GDN_HEADS, GDN_HEADS)

GLA_HEAD_COLS = 2 * GLA_DK + GLA_DV
GDN_HEAD_COLS = 2 * GDN_DK + GDN_DV
OFF_GLA = 0
OFF_AG = OFF_GLA + GLA_HEADS * GLA_HEAD_COLS
OFF_GDN = OFF_AG + GLA_WIDTH
OFF_BZ = OFF_GDN + GDN_HEADS * GDN_HEAD_COLS
OFF_GATES = OFF_BZ + GDN_WIDTH
OFF_LR = OFF_GATES + 128
D_PROJ = OFF_LR + 128
GATE_SLOTS = 8

LANES = 128
GDN_CHUNK = 128
GLA_SUB = 16
GLA_ROWS = 128
VMEM_LIMIT = 56 * 1024 * 1024


def _cparams(sem):
    return pltpu.CompilerParams(dimension_semantics=sem, vmem_limit_bytes=VMEM_LIMIT)


def _dot(a, b):
    return jnp.dot(a.astype(BF16), b.astype(BF16), preferred_element_type=F32)


def _dot_nt(a, b):
    return lax.dot_general(a.astype(BF16), b.astype(BF16), (((1,), (1,)), ((), ())),
                           preferred_element_type=F32)


def _dot_tn(a, b):
    return lax.dot_general(a.astype(BF16), b.astype(BF16), (((0,), (0,)), ((), ())),
                           preferred_element_type=F32)


def _split(a):
    hi = a.astype(BF16)
    lo = (a - hi.astype(F32)).astype(BF16)
    return hi, lo


def _dot3(a, b):
    ah, al = _split(a)
    bh, bl = _split(b)
    d = functools.partial(jnp.dot, preferred_element_type=F32)
    return d(ah, bh) + (d(ah, bl) + d(al, bh))


def _dot_01(mask, b):
    b1 = b.astype(BF16)
    r1 = b - b1.astype(F32)
    b2 = r1.astype(BF16)
    b3 = (r1 - b2.astype(F32)).astype(BF16)
    d = functools.partial(jnp.dot, mask.astype(F32).astype(BF16), preferred_element_type=F32)
    return d(b1) + (d(b2) + d(b3))


def _sigmoid(x):
    return 1.0 / (1.0 + jnp.exp(-x))


def _softplus(x):
    return jnp.maximum(x, 0.0) + jnp.log1p(jnp.exp(-jnp.abs(x)))


def _rms(x):
    return x * lax.rsqrt(jnp.mean(x * x, axis=-1, keepdims=True) + EPS)


def _centred_conv3(x, prev_row, next_row, w):
    rows = x.shape[0]
    rid = lax.broadcasted_iota(jnp.int32, x.shape, 0)
    x_prev = jnp.where(rid == 0, prev_row, pltpu.roll(x, 1, 0))
    x_next = jnp.where(rid == rows - 1, next_row, pltpu.roll(x, rows - 1, 0))
    return x_prev * w[0:1, :] + x * w[1:2, :] + x_next * w[2:3, :]


def _norm_kernel(x_ref, g_ref, o_ref):
    o_ref[...] = (_rms(x_ref[...]) * g_ref[...]).astype(o_ref.dtype)


def _norm(x, gain, tm=256):
    m, d = x.shape
    return pl.pallas_call(
        _norm_kernel,
        grid=(m // tm,),
        in_specs=[pl.BlockSpec((tm, d), lambda i: (i, 0)), pl.BlockSpec((1, d), lambda i: (0, 0))],
        out_specs=pl.BlockSpec((tm, d), lambda i: (i, 0)),
        out_shape=jax.ShapeDtypeStruct((m, d), BF16),
        compiler_params=_cparams(("parallel",)),
        name="rms_norm",
    )(x, gain.reshape(1, d))


def _post_kernel(x_ref, y_ref, gp_ref, gn_ref, xo_ref, ho_ref):
    xn = x_ref[...] + _rms(y_ref[...]) * gp_ref[...]
    xo_ref[...] = xn
    ho_ref[...] = (_rms(xn) * gn_ref[...]).astype(ho_ref.dtype)


def _post_last_kernel(x_ref, y_ref, gp_ref, xo_ref):
    xo_ref[...] = x_ref[...] + _rms(y_ref[...]) * gp_ref[...]


def _post(x, y, g_post, g_next, tm=256):
    m, d = x.shape
    row = pl.BlockSpec((tm, d), lambda i: (i, 0))
    vec = pl.BlockSpec((1, d), lambda i: (0, 0))
    if g_next is None:
        return pl.pallas_call(
            _post_last_kernel, grid=(m // tm,), in_specs=[row, row, vec], out_specs=row,
            out_shape=jax.ShapeDtypeStruct((m, d), F32), compiler_params=_cparams(("parallel",)),
            name="post_last")(x, y, g_post.reshape(1, d)), None
    return pl.pallas_call(
        _post_kernel, grid=(m // tm,), in_specs=[row, row, vec, vec], out_specs=[row, row],
        out_shape=[jax.ShapeDtypeStruct((m, d), F32), jax.ShapeDtypeStruct((m, d), BF16)],
        compiler_params=_cparams(("parallel",)), name="post",
    )(x, y, g_post.reshape(1, d), g_next.reshape(1, d))


def _mm_kernel(a_ref, w_ref, o_ref):
    o_ref[...] = jnp.dot(a_ref[...], w_ref[...], preferred_element_type=F32).astype(o_ref.dtype)


def _mm(a, w, tm, tn, out_dtype=F32, name="mm"):
    m, k = a.shape
    n = w.shape[1]
    return pl.pallas_call(
        _mm_kernel,
        grid=(n // tn, m // tm),
        in_specs=[pl.BlockSpec((tm, k), lambda j, i: (i, 0)), pl.BlockSpec((k, tn), lambda j, i: (0, j))],
        out_specs=pl.BlockSpec((tm, tn), lambda j, i: (i, j)),
        out_shape=jax.ShapeDtypeStruct((m, n), out_dtype),
        compiler_params=_cparams(("parallel", "parallel")),
        name=name,
    )(a, w)


def _mm_ksplit_kernel(a_ref, w_ref, o_ref, acc_ref, *, nk):
    kk = pl.program_id(2)

    @pl.when(kk == 0)
    def _():
        acc_ref[...] = jnp.zeros_like(acc_ref)

    acc_ref[...] += jnp.dot(a_ref[...], w_ref[...], preferred_element_type=F32)

    @pl.when(kk == nk - 1)
    def _():
        o_ref[...] = acc_ref[...]


def _mm_ksplit(a, w, tm, tn, tk, name="mm_ksplit"):
    m, k = a.shape
    n = w.shape[1]
    nk = k // tk
    return pl.pallas_call(
        functools.partial(_mm_ksplit_kernel, nk=nk),
        grid=(m // tm, n // tn, nk),
        in_specs=[pl.BlockSpec((tm, tk), lambda i, j, kk: (i, kk)),
                  pl.BlockSpec((tk, tn), lambda i, j, kk: (kk, j))],
        out_specs=pl.BlockSpec((tm, tn), lambda i, j, kk: (i, j)),
        out_shape=jax.ShapeDtypeStruct((m, n), F32),
        scratch_shapes=[pltpu.VMEM((tm, tn), F32)],
        compiler_params=_cparams(("parallel", "parallel", "arbitrary")),
        name=name,
    )(a, w)


HALO = 16


def _ffn_up_kernel(h_ref, hp_ref, hn_ref, wg_ref, wu_ref, cwg_ref, cwu_ref, cbg_ref, cbu_ref, o_ref,
                   *, tm, nm):
    i = pl.program_id(1)
    lhs = jnp.concatenate([hp_ref[...], h_ref[...], hn_ref[...]], axis=0)
    not_first = (i > 0).astype(F32)
    not_last = (i < nm - 1).astype(F32)

    def branch(w_ref, cw_ref, cb_ref):
        y = jnp.dot(lhs, w_ref[...], preferred_element_type=F32)
        prev_row = y[HALO - 1:HALO, :] * not_first
        next_row = y[HALO + tm:HALO + tm + 1, :] * not_last
        return _centred_conv3(y[HALO:HALO + tm, :], prev_row, next_row, cw_ref[...]) + cb_ref[...]

    gate = branch(wg_ref, cwg_ref, cbg_ref)
    up = branch(wu_ref, cwu_ref, cbu_ref)
    c0 = np.float32(np.sqrt(2.0 / np.pi))
    gelu = 0.5 * gate * (1.0 + jnp.tanh(c0 * (gate + np.float32(0.044715) * (gate * gate * gate))))
    o_ref[...] = (gelu * up).astype(o_ref.dtype)


def _ffn_up(h, w_up, conv_w, conv_b, tm=512, tn=512):
    m, k = h.shape
    nm, nn = m // tm, D_FF // tn
    rb = tm // HALO
    nhb = m // HALO
    conv_b = conv_b.reshape(1, 2 * D_FF)
    return pl.pallas_call(
        functools.partial(_ffn_up_kernel, tm=tm, nm=nm),
        grid=(nn, nm),
        in_specs=[
            pl.BlockSpec((tm, k), lambda j, i: (i, 0)),
            pl.BlockSpec((HALO, k), lambda j, i: (jnp.maximum(i * rb - 1, 0), 0)),
            pl.BlockSpec((HALO, k), lambda j, i: (jnp.minimum((i + 1) * rb, nhb - 1), 0)),
            pl.BlockSpec((k, tn), lambda j, i: (0, j)),
            pl.BlockSpec((k, tn), lambda j, i: (0, j + nn)),
            pl.BlockSpec((3, tn), lambda j, i: (0, j)),
            pl.BlockSpec((3, tn), lambda j, i: (0, j + nn)),
            pl.BlockSpec((1, tn), lambda j, i: (0, j)),
            pl.BlockSpec((1, tn), lambda j, i: (0, j + nn)),
        ],
        out_specs=pl.BlockSpec((tm, tn), lambda j, i: (i, j)),
        out_shape=jax.ShapeDtypeStruct((m, D_FF), BF16),
        compiler_params=_cparams(("parallel", "parallel")),
        name="ffn_up",
    )(h, h, h, w_up, w_up, conv_w, conv_w, conv_b, conv_b)


def _ple_kernel(u_ref, wg_ref, p_ref, wp_ref, o_ref):
    gate = _sigmoid(jnp.dot(u_ref[...], wg_ref[...], preferred_element_type=F32))
    proj = jnp.dot(p_ref[...].astype(BF16), wp_ref[...], preferred_element_type=F32)
    o_ref[...] = proj * gate


def _ple(u, w_gate, p, w_proj, tm=512, tn=1024):
    m, k = u.shape
    n = w_gate.shape[1]
    return pl.pallas_call(
        _ple_kernel,
        grid=(n // tn, m // tm),
        in_specs=[pl.BlockSpec((tm, k), lambda j, i: (i, 0)), pl.BlockSpec((k, tn), lambda j, i: (0, j)),
                  pl.BlockSpec((tm, PLE_DIM), lambda j, i: (i, 0)),
                  pl.BlockSpec((PLE_DIM, tn), lambda j, i: (0, j))],
        out_specs=pl.BlockSpec((tm, tn), lambda j, i: (i, j)),
        out_shape=jax.ShapeDtypeStruct((m, n), F32),
        compiler_params=_cparams(("parallel", "parallel")),
        name="ple_gate",
    )(u, w_gate, p, w_proj)


def _gdn_gates_kernel(x_ref, alog_ref, dtb_ref, o_ref):
    x = x_ref[...]
    n = x.shape[0]
    slot = lax.broadcasted_iota(jnp.int32, x.shape, 1) % GATE_SLOTS
    decay = -jnp.exp(alog_ref[...]) * _softplus(x + dtb_ref[...])
    ri = lax.broadcasted_iota(jnp.int32, (n, n), 0)
    ci = lax.broadcasted_iota(jnp.int32, (n, n), 1)
    prefix = _dot_01(ci <= ri, decay)
    suffix = _dot_01(ci >= ri, decay)
    vals = jnp.where(slot < 2, _sigmoid(x), jnp.where(slot == 2, prefix, suffix))
    o_ref[...] = vals.T


def _gdn_gates(proj, alog_lanes, dtb_lanes):
    m = proj.shape[0]
    cb = OFF_GATES // LANES
    return pl.pallas_call(
        _gdn_gates_kernel,
        grid=(m // GDN_CHUNK,),
        in_specs=[pl.BlockSpec((GDN_CHUNK, LANES), lambda c: (c, cb)),
                  pl.BlockSpec((1, LANES), lambda c: (0, 0)),
                  pl.BlockSpec((1, LANES), lambda c: (0, 0))],
        out_specs=pl.BlockSpec((LANES, GDN_CHUNK), lambda c: (0, c)),
        out_shape=jax.ShapeDtypeStruct((LANES, m), F32),
        compiler_params=_cparams(("parallel",)),
        name="gdn_gates",
    )(proj, alog_lanes, dtb_lanes)


GDN_INV_SPLIT_STAGES = 0
GDN_INV_REFINE = True


def _inv_dot(a, b, stage):
    return _dot3(a, b) if stage < GDN_INV_SPLIT_STAGES else _dot(a, b)


def _gdn_chains(chains):
    c = chains[0][0].shape[0]
    ri = lax.broadcasted_iota(jnp.int32, (c, c), 0)
    ci = lax.broadcasted_iota(jnp.int32, (c, c), 1)
    eye = (ri == ci).astype(F32)
    incl = {False: ci <= ri, True: ci >= ri}
    strict = {False: ci < ri, True: ci > ri}
    n = len(chains)
    rng = range(n)
    qs = [ch[0] for ch in chains]
    ks = [ch[1] for ch in chains]
    vs = [ch[2] for ch in chains]
    ss = [ch[5] for ch in chains]
    revs = [ch[6] for ch in chains]
    g_row = [jnp.broadcast_to(ch[4], (c, c)) for ch in chains]
    g_col = [g.T for g in g_row]
    beta_col = [jnp.broadcast_to(ch[3], (c, c)).T for ch in chains]
    gamma = [jnp.where(incl[revs[i]], jnp.exp(jnp.where(incl[revs[i]], g_col[i] - g_row[i], 0.0)), 0.0)
             for i in rng]
    e_col = [jnp.exp(g) for g in g_col]
    kb = [ks[i] * beta_col[i] for i in rng]
    sc = [_dot_nt(jnp.concatenate([kb[i], qs[i]], axis=0), ks[i]) for i in rng]
    a = [jnp.where(strict[revs[i]], sc[i][:c] * gamma[i], 0.0) for i in rng]
    qk = [jnp.where(incl[revs[i]], sc[i][c:] * gamma[i], 0.0) for i in rng]
    p = [-x for x in a]
    t = [eye + x for x in p]
    p = [_inv_dot(x, x, 0) for x in p]
    n_stacked = int(np.log2(c)) - 2
    for stage in range(1, n_stacked + 1):
        pt = [_inv_dot(jnp.concatenate([p[i], t[i]], axis=0), p[i], stage) for i in rng]
        t = [t[i] + pt[i][c:] for i in rng]
        p = [pt[i][:c] for i in rng]
    t = [t[i] + _inv_dot(t[i], p[i], n_stacked + 1) for i in rng]
    if GDN_INV_REFINE:
        res = [eye - t[i] - _dot3(a[i], t[i]) for i in rng]
        t = [t[i] + _dot(t[i], res[i]) for i in rng]
    uw = [_dot(t[i], jnp.concatenate([vs[i] * beta_col[i], kb[i] * e_col[i]], axis=1)) for i in rng]
    d = vs[0].shape[1]
    ws = [_dot(jnp.concatenate([uw[i][:, d:], qs[i] * e_col[i]], axis=0), ss[i]) for i in rng]
    v_new = [uw[i][:, :d] - ws[i][:c] for i in rng]
    g_last = [ch[4][:, 0:1] if ch[6] else ch[4][:, c - 1:c] for ch in chains]
    k_dec_t = [(ks[i] * jnp.exp(g_last[i] - g_col[i])).T for i in rng]
    ov = [_dot(jnp.concatenate([qk[i], k_dec_t[i]], axis=0), v_new[i]) for i in rng]
    return [(ws[i][c:] + ov[i][:c], jnp.exp(g_last[i]) * ss[i] + ov[i][c:]) for i in rng]


def _gdn_kernel(xf_ref, xfp_ref, xfn_ref, xb_ref, xbp_ref, xbn_ref, cw_ref, gf_ref, gb_ref,
                of_ref, ob_ref, sf_ref, sb_ref, *, hp, nch):
    c = pl.program_id(1)

    @pl.when(c == 0)
    def _():
        sf_ref[...] = jnp.zeros_like(sf_ref)
        sb_ref[...] = jnp.zeros_like(sb_ref)

    cw = cw_ref[...]
    dirs = ((xf_ref, xfp_ref, xfn_ref, gf_ref, of_ref, sf_ref, False, c),
            (xb_ref, xbp_ref, xbn_ref, gb_ref, ob_ref, sb_ref, True, nch - 1 - c))
    chains, sinks = [], []
    for x_ref, xp_ref, xn_ref, g_ref, o_ref, s_ref, rev, chunk in dirs:
        prev_row = xp_ref[7:8, :] * (chunk > 0).astype(F32)
        next_row = xn_ref[0:1, :] * (chunk < nch - 1).astype(F32)
        y = _centred_conv3(x_ref[...], prev_row, next_row, cw)
        y = y * _sigmoid(y)
        for h in range(hp):
            b0 = h * GDN_HEAD_COLS
            q = y[:, b0:b0 + GDN_DK]
            k = y[:, b0 + GDN_DK:b0 + 2 * GDN_DK]
            v = y[:, b0 + 2 * GDN_DK:b0 + GDN_HEAD_COLS]
            q = q * lax.rsqrt(jnp.sum(q * q, axis=-1, keepdims=True) + EPS) * np.float32(GDN_DK ** -0.5)
            k = k * lax.rsqrt(jnp.sum(k * k, axis=-1, keepdims=True) + EPS)
            g0 = h * GATE_SLOTS
            beta_row = g_ref[g0 + 1:g0 + 2, :] if rev else g_ref[g0:g0 + 1, :]
            gc_row = g_ref[g0 + 3:g0 + 4, :] if rev else g_ref[g0 + 2:g0 + 3, :]
            chains.append((q, k, v, beta_row, gc_row, s_ref[h], rev))
            sinks.append((o_ref, s_ref, h))
    for (o, s_new), (o_ref, s_ref, h) in zip(_gdn_chains(chains), sinks):
        s_ref[h] = s_new
        o_ref[:, h * GDN_DV:(h + 1) * GDN_DV] = o


def _gdn(proj, conv_w_perm, gates_t, hp=4):
    m = proj.shape[0]
    nch = m // GDN_CHUNK
    width = hp * GDN_HEAD_COLS
    cb0 = OFF_GDN // width
    r8 = GDN_CHUNK // 8
    n8 = m // 8
    fwd = lambda g, c: c
    bwd = lambda g, c: nch - 1 - c

    def main(ch):
        return pl.BlockSpec((GDN_CHUNK, width), lambda g, c: (ch(g, c), cb0 + g))

    def prev8(ch):
        return pl.BlockSpec((8, width), lambda g, c: (jnp.maximum(ch(g, c) * r8 - 1, 0), cb0 + g))

    def next8(ch):
        return pl.BlockSpec((8, width), lambda g, c: (jnp.minimum((ch(g, c) + 1) * r8, n8 - 1), cb0 + g))

    def gates(ch):
        return pl.BlockSpec((hp * GATE_SLOTS, GDN_CHUNK), lambda g, c: (g, ch(g, c)))

    def out(ch):
        return pl.BlockSpec((GDN_CHUNK, hp * GDN_DV), lambda g, c: (ch(g, c), g))

    return pl.pallas_call(
        functools.partial(_gdn_kernel, hp=hp, nch=nch),
        grid=(GDN_HEADS // hp, nch),
        in_specs=[main(fwd), prev8(fwd), next8(fwd), main(bwd), prev8(bwd), next8(bwd),
                  pl.BlockSpec((3, width), lambda g, c: (0, g)), gates(fwd), gates(bwd)],
        out_specs=[out(fwd), out(bwd)],
        out_shape=[jax.ShapeDtypeStruct((m, GDN_WIDTH), F32)] * 2,
        scratch_shapes=[pltpu.VMEM((hp, GDN_DK, GDN_DV), F32)] * 2,
        compiler_params=_cparams(("parallel", "arbitrary")),
        name="gdn",
    )(proj, proj, proj, proj, proj, proj, conv_w_perm, gates_t, gates_t)


def _gla_dir(x, lr, w2, b2, st_ref, o_ref, rev):
    rows = x.shape[0]
    nsub = rows // GLA_SUB
    logit = _dot3(lr, w2) + b2
    g = (jnp.minimum(logit, 0.0) - jnp.log1p(jnp.exp(-jnp.abs(logit)))) * np.float32(1.0 / GLA_GATE_NORM)
    q = x[:, 0:GLA_DK] * np.float32(GLA_DK ** -0.5)
    k = x[:, GLA_DK:2 * GLA_DK]
    v = x[:, 2 * GLA_DK:GLA_HEAD_COLS]

    ri = lax.broadcasted_iota(jnp.int32, (rows, rows), 0)
    ci = lax.broadcasted_iota(jnp.int32, (rows, rows), 1)
    b = _dot_01((ci >= ri) if rev else (ci <= ri), g)
    b_tot = b[0:1, :] if rev else b[rows - 1:rows, :]
    st = st_ref[...]

    o = _dot_nt(q * jnp.exp(b), st)

    rid = lax.broadcasted_iota(jnp.int32, k.shape, 0)
    blocks = []
    for s in range(nsub):
        r0 = s * GLA_SUB
        if (s == nsub - 1) if rev else (s == 0):
            blocks.append(jnp.zeros((GLA_SUB, rows), F32))
            continue
        ref_row = b[r0 + GLA_SUB - 1:r0 + GLA_SUB, :] if rev else b[r0:r0 + 1, :]
        q_t = q[r0:r0 + GLA_SUB, :] * jnp.exp(b[r0:r0 + GLA_SUB, :] - ref_row)
        earlier = (rid >= r0 + GLA_SUB) if rev else (rid < r0)
        k_t = jnp.where(earlier, k * jnp.exp(jnp.minimum(ref_row - b, 0.0)), 0.0)
        blocks.append(_dot_nt(q_t, k_t))
    o = o + _dot(jnp.concatenate(blocks, axis=0), v)

    b3 = b.reshape(nsub, GLA_SUB, GLA_DK)
    q3 = q.reshape(nsub, GLA_SUB, GLA_DK)
    k3 = k.reshape(nsub, GLA_SUB, GLA_DK)
    tiles = tuple((lo, lo + 8) for lo in range(0, GLA_SUB, 8))
    lane = lax.broadcasted_iota(jnp.int32, (nsub, 8, GLA_SUB), 2)
    att_t = [jnp.zeros((nsub, 8, GLA_SUB), F32) for _ in tiles]
    for j in range(GLA_SUB):
        for ti, (lo, hi) in enumerate(tiles):
            if (lo > j) if rev else (hi - 1 < j):
                continue
            e = jnp.exp(jnp.minimum(b3[:, lo:hi, :] - b3[:, j:j + 1, :], 0.0))
            col = jnp.sum(q3[:, lo:hi, :] * e * k3[:, j:j + 1, :], axis=-1, keepdims=True)
            att_t[ti] = jnp.where(lane == j, col, att_t[ti])
    att = jnp.concatenate(att_t, axis=1)
    lane = lax.broadcasted_iota(jnp.int32, att.shape, 2)
    sub = lax.broadcasted_iota(jnp.int32, att.shape, 1)
    att = jnp.where((lane >= sub) if rev else (lane <= sub), att, 0.0)

    for s in range(nsub):
        r0 = s * GLA_SUB
        o_ref[r0:r0 + GLA_SUB, :] = o[r0:r0 + GLA_SUB, :] + _dot(att[s], v[r0:r0 + GLA_SUB, :])

    st_ref[...] = st * jnp.exp(b_tot) + _dot(v.T, k * jnp.exp(b_tot - b))


def _gla_kernel(xf_ref, xb_ref, lrf_ref, lrb_ref, w2f_ref, w2b_ref, b2f_ref, b2b_ref,
                of_ref, ob_ref, sf_ref, sb_ref):
    c = pl.program_id(1)

    @pl.when(c == 0)
    def _():
        sf_ref[...] = jnp.zeros_like(sf_ref)
        sb_ref[...] = jnp.zeros_like(sb_ref)

    _gla_dir(xf_ref[...], lrf_ref[...], w2f_ref[...], b2f_ref[...], sf_ref, of_ref, False)
    _gla_dir(xb_ref[...], lrb_ref[...], w2b_ref[...], b2b_ref[...], sb_ref, ob_ref, True)


def _gla(proj, w2f_pad, w2b_pad, b2f, b2b):
    m = proj.shape[0]
    nch = m // GLA_ROWS
    lr_cb = OFF_LR // LANES
    fwd = lambda h, c: c
    bwd = lambda h, c: nch - 1 - c

    def spec(cols, ch, col_block):
        return pl.BlockSpec((GLA_ROWS, cols), lambda h, c: (ch(h, c), col_block(h)))

    return pl.pallas_call(
        _gla_kernel,
        grid=(GLA_HEADS, nch),
        in_specs=[spec(GLA_HEAD_COLS, fwd, lambda h: h), spec(GLA_HEAD_COLS, bwd, lambda h: h),
                  spec(LANES, fwd, lambda h: lr_cb), spec(LANES, bwd, lambda h: lr_cb),
                  pl.BlockSpec((LANES, GLA_DK), lambda h, c: (0, h)),
                  pl.BlockSpec((LANES, GLA_DK), lambda h, c: (0, h)),
                  pl.BlockSpec((1, GLA_DK), lambda h, c: (0, h)),
                  pl.BlockSpec((1, GLA_DK), lambda h, c: (0, h))],
        out_specs=[spec(GLA_DV, fwd, lambda h: h), spec(GLA_DV, bwd, lambda h: h)],
        out_shape=[jax.ShapeDtypeStruct((m, GLA_WIDTH), F32)] * 2,
        scratch_shapes=[pltpu.VMEM((GLA_DV, GLA_DK), F32)] * 2,
        compiler_params=_cparams(("parallel", "arbitrary")),
        name="gla",
    )(proj, proj, proj, proj, w2f_pad, w2b_pad, b2f, b2b)


def _mix_out_kernel(af_ref, ab_ref, ag_ref, bf_ref, bb_ref, bz_ref, na_ref, nb_ref, o_ref):
    def heads(f_ref, b_ref, z_ref, gain_ref, n_heads, width, base):
        for h in range(n_heads):
            sl = slice(h * width, (h + 1) * width)
            o = _rms(f_ref[:, sl] + b_ref[:, sl]) * gain_ref[...]
            z = z_ref[:, sl]
            o_ref[:, base + h * width:base + (h + 1) * width] = (o * (z * _sigmoid(z))).astype(o_ref.dtype)

    heads(af_ref, ab_ref, ag_ref, na_ref, GLA_HEADS, GLA_DV, 0)
    heads(bf_ref, bb_ref, bz_ref, nb_ref, GDN_HEADS, GDN_DV, GLA_WIDTH)


def _mix_out(oaf, oab, obf, obb, proj, gla_norm, gdn_norm, tm=256):
    m = proj.shape[0]
    half = pl.BlockSpec((tm, GLA_WIDTH), lambda i: (i, 0))
    return pl.pallas_call(
        _mix_out_kernel,
        grid=(m // tm,),
        in_specs=[half, half, pl.BlockSpec((tm, GLA_WIDTH), lambda i: (i, OFF_AG // GLA_WIDTH)),
                  half, half, pl.BlockSpec((tm, GDN_WIDTH), lambda i: (i, OFF_BZ // GDN_WIDTH)),
                  pl.BlockSpec((1, GLA_DV), lambda i: (0, 0)), pl.BlockSpec((1, GDN_DV), lambda i: (0, 0))],
        out_specs=pl.BlockSpec((tm, D_MODEL), lambda i: (i, 0)),
        out_shape=jax.ShapeDtypeStruct((m, D_MODEL), BF16),
        compiler_params=_cparams(("parallel",)),
        name="mix_out",
    )(oaf, oab, proj, obf, obb, proj, gla_norm.reshape(1, GLA_DV), gdn_norm.reshape(1, GDN_DV))


def _permute_w_in(w):
    kdim = w.shape[0]
    idx = np.cumsum(IN_SPLITS)[:-1].tolist()
    (a_q, a_k, a_v, a_g, lr_f, lr_b, b_qkv, b_z, beta_f, beta_b, dec_f, dec_b) = jnp.split(w, idx, axis=1)
    gla = jnp.concatenate([a_q.reshape(kdim, GLA_HEADS, GLA_DK), a_k.reshape(kdim, GLA_HEADS, GLA_DK),
                           a_v.reshape(kdim, GLA_HEADS, GLA_DV)], axis=2).reshape(kdim, -1)
    d_q, d_k, d_v = jnp.split(b_qkv, [GDN_KEY, 2 * GDN_KEY], axis=1)
    gdn = jnp.stack([d_q.reshape(kdim, GDN_HEADS, GDN_DK), d_k.reshape(kdim, GDN_HEADS, GDN_DK),
                     d_v.reshape(kdim, GDN_HEADS, GDN_DV)], axis=2).reshape(kdim, -1)
    gates = jnp.stack([beta_f, beta_b, dec_f, dec_b], axis=2)
    gates = jnp.pad(gates, ((0, 0), (0, 0), (0, GATE_SLOTS - 4))).reshape(kdim, GDN_HEADS * GATE_SLOTS)
    lr = jnp.pad(jnp.concatenate([lr_f, lr_b], axis=1), ((0, 0), (0, LANES - 2 * GLA_RANK)))
    return jnp.concatenate([gla, a_g, gdn, b_z, gates, lr], axis=1).astype(BF16)


def _permute_gdn_conv(cw):
    d_q, d_k, d_v = jnp.split(cw, [GDN_KEY, 2 * GDN_KEY], axis=1)
    return jnp.stack([d_q.reshape(3, GDN_HEADS, GDN_DK), d_k.reshape(3, GDN_HEADS, GDN_DK),
                      d_v.reshape(3, GDN_HEADS, GDN_DV)], axis=2).reshape(3, -1)


def _gate_lanes(f, b):
    z = jnp.zeros_like(f)
    return jnp.stack([z, z, f, b, z, z, z, z], axis=1).reshape(1, GDN_HEADS * GATE_SLOTS)


def _pad_gate_w(w, first_row):
    return jnp.zeros((LANES, GLA_KEY), F32).at[first_row:first_row + GLA_RANK, :].set(w)


def kernel(x, p, norm_mix_pre, w_in, gla_gate_w_f, gla_gate_b_f, gla_gate_w_b, gla_gate_b_b, gla_out_norm, gdn_conv, gdn_a_log_f, gdn_dt_bias_f, gdn_a_log_b, gdn_dt_bias_b, gdn_out_norm, w_out, norm_mix_post, norm_ffn_pre, ffn_w_up, ffn_conv_w, ffn_conv_b, ffn_w_down, norm_ffn_post, ple_w_proj, ple_w_gate, norm_ple_post):
    bsz, seq, d = x.shape
    depth = w_in.shape[0]
    m = bsz * seq
    assert bsz == 1, "the scans treat the flattened rows as one sequence"
    xm = x.reshape(m, d)
    h = _norm(xm, norm_mix_pre[0])
    for i in range(depth):
        proj = _mm(h, _permute_w_in(w_in[i]), tm=512, tn=768, name="in_proj")
        gates_t = _gdn_gates(proj, _gate_lanes(gdn_a_log_f[i], gdn_a_log_b[i]),
                             _gate_lanes(gdn_dt_bias_f[i], gdn_dt_bias_b[i]))
        oaf, oab = _gla(proj, _pad_gate_w(gla_gate_w_f[i], 0), _pad_gate_w(gla_gate_w_b[i], GLA_RANK),
                        gla_gate_b_f[i].reshape(1, GLA_KEY), gla_gate_b_b[i].reshape(1, GLA_KEY))
        obf, obb = _gdn(proj, _permute_gdn_conv(gdn_conv[i]), gates_t)
        o = _mix_out(oaf, oab, obf, obb, proj, gla_out_norm[i], gdn_out_norm[i])
        mix = _mm(o, w_out[i].astype(BF16), tm=512, tn=1024, name="out_proj")
        xm, h = _post(xm, mix, norm_mix_post[i], norm_ffn_pre[i])
        act = _ffn_up(h, ffn_w_up[i].astype(BF16), ffn_conv_w[i], ffn_conv_b[i])
        ffn = _mm_ksplit(act, ffn_w_down[i].astype(BF16), tm=1024, tn=1024, tk=2048, name="ffn_down")
        xm, u = _post(xm, ffn, norm_ffn_post[i], jnp.ones((d,), F32))
        t = _ple(u, ple_w_gate[i].astype(BF16), p[i].reshape(m, PLE_DIM), ple_w_proj[i].astype(BF16))
        g_next = norm_mix_pre[i + 1] if i + 1 < depth else None
        xm, h = _post(xm, t, norm_ple_post[i], g_next)
    return xm.reshape(bsz, seq, d)
```

```python
import functools

import numpy as np
import jax
import jax.numpy as jnp
from jax import lax
from jax.experimental import pallas as pl
from jax.experimental.pallas import tpu as pltpu

F32 = jnp.float32
BF16 = jnp.bfloat16
HIGHEST = lax.Precision.HIGHEST

D_MODEL = 4096
PLE_DIM = 256
GLA_HEADS, GLA_DK, GLA_DV, GLA_RANK = 4, 256, 512, 16
GLA_KEY = GLA_HEADS * GLA_DK
GLA_WIDTH = GLA_HEADS * GLA_DV
GLA_GATE_NORM = 16.0
GDN_HEADS, GDN_DK, GDN_DV = 16, 128, 128
GDN_KEY = GDN_HEADS * GDN_DK
GDN_WIDTH = GDN_HEADS * GDN_DV
D_FF = 4 * D_MODEL
EPS = 1e-6
IN_SPLITS = (GLA_KEY, GLA_KEY, GLA_WIDTH, GLA_WIDTH, GLA_RANK, GLA_RANK,
             2 * GDN_KEY + GDN_WIDTH, GDN_WIDTH, GDN_HEADS, GDN_HEADS, GDN_HEADS, GDN_HEADS)

GLA_HEAD_COLS = 2 * GLA_DK + GLA_DV
GDN_HEAD_COLS = 2 * GDN_DK + GDN_DV
OFF_GLA = 0
OFF_AG = OFF_GLA + GLA_HEADS * GLA_HEAD_COLS
OFF_GDN = OFF_AG + GLA_WIDTH
OFF_BZ = OFF_GDN + GDN_HEADS * GDN_HEAD_COLS
OFF_GATES = OFF_BZ + GDN_WIDTH
OFF_LR = OFF_GATES + 128
D_PROJ = OFF_LR + 128
GATE_SLOTS = 8

LANES = 128
GDN_CHUNK = 128
GLA_SUB = 16
GLA_ROWS = 256
VMEM_LIMIT = 56 * 1024 * 1024


def _cparams(sem):
    return pltpu.CompilerParams(dimension_semantics=sem, vmem_limit_bytes=VMEM_LIMIT)


def _dot(a, b):
    return jnp.dot(a.astype(BF16), b.astype(BF16), preferred_element_type=F32)


def _dot_nt(a, b):
    return lax.dot_general(a.astype(BF16), b.astype(BF16), (((1,), (1,)), ((), ())),
                           preferred_element_type=F32)


def _dot_tn(a, b):
    return lax.dot_general(a.astype(BF16), b.astype(BF16), (((0,), (0,)), ((), ())),
                           preferred_element_type=F32)


def _split(a):
    hi = a.astype(BF16)
    lo = (a - hi.astype(F32)).astype(BF16)
    return hi, lo


def _dot3(a, b):
    ah, al = _split(a)
    bh, bl = _split(b)
    d = functools.partial(jnp.dot, preferred_element_type=F32)
    return d(ah, bh) + (d(ah, bl) + d(al, bh))


def _dot_01(mask, b):
    b1 = b.astype(BF16)
    r1 = b - b1.astype(F32)
    b2 = r1.astype(BF16)
    b3 = (r1 - b2.astype(F32)).astype(BF16)
    d = functools.partial(jnp.dot, mask.astype(F32).astype(BF16), preferred_element_type=F32)
    return d(b1) + (d(b2) + d(b3))


def _sigmoid(x):
    return 1.0 / (1.0 + jnp.exp(-x))


def _softplus(x):
    return jnp.maximum(x, 0.0) + jnp.log1p(jnp.exp(-jnp.abs(x)))


def _rms(x):
    return x * lax.rsqrt(jnp.mean(x * x, axis=-1, keepdims=True) + EPS)


def _centred_conv3(x, prev_row, next_row, w):
    rows = x.shape[0]
    rid = lax.broadcasted_iota(jnp.int32, x.shape, 0)
    x_prev = jnp.where(rid == 0, prev_row, pltpu.roll(x, 1, 0))
    x_next = jnp.where(rid == rows - 1, next_row, pltpu.roll(x, rows - 1, 0))
    return x_prev * w[0:1, :] + x * w[1:2, :] + x_next * w[2:3, :]


def _norm_kernel(x_ref, g_ref, o_ref):
    o_ref[...] = (_rms(x_ref[...]) * g_ref[...]).astype(o_ref.dtype)


def _norm(x, gain, tm=256):
    m, d = x.shape
    return pl.pallas_call(
        _norm_kernel,
        grid=(m // tm,),
        in_specs=[pl.BlockSpec((tm, d), lambda i: (i, 0)), pl.BlockSpec((1, d), lambda i: (0, 0))],
        out_specs=pl.BlockSpec((tm, d), lambda i: (i, 0)),
        out_shape=jax.ShapeDtypeStruct((m, d), BF16),
        compiler_params=_cparams(("parallel",)),
        name="rms_norm",
    )(x, gain.reshape(1, d))


def _post_kernel(x_ref, y_ref, gp_ref, gn_ref, xo_ref, ho_ref):
    xn = x_ref[...] + _rms(y_ref[...]) * gp_ref[...]
    xo_ref[...] = xn
    ho_ref[...] = (_rms(xn) * gn_ref[...]).astype(ho_ref.dtype)


def _post_last_kernel(x_ref, y_ref, gp_ref, xo_ref):
    xo_ref[...] = x_ref[...] + _rms(y_ref[...]) * gp_ref[...]


def _post(x, y, g_post, g_next, tm=256):
    m, d = x.shape
    row = pl.BlockSpec((tm, d), lambda i: (i, 0))
    vec = pl.BlockSpec((1, d), lambda i: (0, 0))
    if g_next is None:
        return pl.pallas_call(
            _post_last_kernel, grid=(m // tm,), in_specs=[row, row, vec], out_specs=row,
            out_shape=jax.ShapeDtypeStruct((m, d), F32), compiler_params=_cparams(("parallel",)),
            name="post_last")(x, y, g_post.reshape(1, d)), None
    return pl.pallas_call(
        _post_kernel, grid=(m // tm,), in_specs=[row, row, vec, vec], out_specs=[row, row],
        out_shape=[jax.ShapeDtypeStruct((m, d), F32), jax.ShapeDtypeStruct((m, d), BF16)],
        compiler_params=_cparams(("parallel",)), name="post",
    )(x, y, g_post.reshape(1, d), g_next.reshape(1, d))


def _mm_kernel(a_ref, w_ref, o_ref):
    o_ref[...] = jnp.dot(a_ref[...], w_ref[...], preferred_element_type=F32).astype(o_ref.dtype)


def _mm(a, w, tm, tn, out_dtype=F32, name="mm"):
    m, k = a.shape
    n = w.shape[1]
    return pl.pallas_call(
        _mm_kernel,
        grid=(n // tn, m // tm),
        in_specs=[pl.BlockSpec((tm, k), lambda j, i: (i, 0)), pl.BlockSpec((k, tn), lambda j, i: (0, j))],
        out_specs=pl.BlockSpec((tm, tn), lambda j, i: (i, j)),
        out_shape=jax.ShapeDtypeStruct((m, n), out_dtype),
        compiler_params=_cparams(("parallel", "parallel")),
        name=name,
    )(a, w)


def _mm_ksplit_kernel(a_ref, w_ref, o_ref, acc_ref, *, nk):
    kk = pl.program_id(2)

    @pl.when(kk == 0)
    def _():
        acc_ref[...] = jnp.zeros_like(acc_ref)

    acc_ref[...] += jnp.dot(a_ref[...], w_ref[...], preferred_element_type=F32)

    @pl.when(kk == nk - 1)
    def _():
        o_ref[...] = acc_ref[...]


def _mm_ksplit(a, w, tm, tn, tk, name="mm_ksplit"):
    m, k = a.shape
    n = w.shape[1]
    nk = k // tk
    return pl.pallas_call(
        functools.partial(_mm_ksplit_kernel, nk=nk),
        grid=(m // tm, n // tn, nk),
        in_specs=[pl.BlockSpec((tm, tk), lambda i, j, kk: (i, kk)),
                  pl.BlockSpec((tk, tn), lambda i, j, kk: (kk, j))],
        out_specs=pl.BlockSpec((tm, tn), lambda i, j, kk: (i, j)),
        out_shape=jax.ShapeDtypeStruct((m, n), F32),
        scratch_shapes=[pltpu.VMEM((tm, tn), F32)],
        compiler_params=_cparams(("parallel", "parallel", "arbitrary")),
        name=name,
    )(a, w)


HALO = 16


def _ffn_up_kernel(h_ref, hp_ref, hn_ref, wg_ref, wu_ref, cwg_ref, cwu_ref, cbg_ref, cbu_ref, o_ref,
                   *, tm, nm):
    i = pl.program_id(1)
    lhs = jnp.concatenate([hp_ref[...], h_ref[...], hn_ref[...]], axis=0)
    not_first = (i > 0).astype(F32)
    not_last = (i < nm - 1).astype(F32)

    def branch(w_ref, cw_ref, cb_ref):
        y = jnp.dot(lhs, w_ref[...], preferred_element_type=F32)
        prev_row = y[HALO - 1:HALO, :] * not_first
        next_row = y[HALO + tm:HALO + tm + 1, :] * not_last
        return _centred_conv3(y[HALO:HALO + tm, :], prev_row, next_row, cw_ref[...]) + cb_ref[...]

    gate = branch(wg_ref, cwg_ref, cbg_ref)
    up = branch(wu_ref, cwu_ref, cbu_ref)
    c0 = np.float32(np.sqrt(2.0 / np.pi))
    gelu = 0.5 * gate * (1.0 + jnp.tanh(c0 * (gate + np.float32(0.044715) * (gate * gate * gate))))
    o_ref[...] = (gelu * up).astype(o_ref.dtype)


def _ffn_up(h, w_up, conv_w, conv_b, tm=1024, tn=512):
    m, k = h.shape
    nm, nn = m // tm, D_FF // tn
    rb = tm // HALO
    nhb = m // HALO
    conv_b = conv_b.reshape(1, 2 * D_FF)
    return pl.pallas_call(
        functools.partial(_ffn_up_kernel, tm=tm, nm=nm),
        grid=(nn, nm),
        in_specs=[
            pl.BlockSpec((tm, k), lambda j, i: (i, 0)),
            pl.BlockSpec((HALO, k), lambda j, i: (jnp.maximum(i * rb - 1, 0), 0)),
            pl.BlockSpec((HALO, k), lambda j, i: (jnp.minimum((i + 1) * rb, nhb - 1), 0)),
            pl.BlockSpec((k, tn), lambda j, i: (0, j)),
            pl.BlockSpec((k, tn), lambda j, i: (0, j + nn)),
            pl.BlockSpec((3, tn), lambda j, i: (0, j)),
            pl.BlockSpec((3, tn), lambda j, i: (0, j + nn)),
            pl.BlockSpec((1, tn), lambda j, i: (0, j)),
            pl.BlockSpec((1, tn), lambda j, i: (0, j + nn)),
        ],
        out_specs=pl.BlockSpec((tm, tn), lambda j, i: (i, j)),
        out_shape=jax.ShapeDtypeStruct((m, D_FF), BF16),
        compiler_params=_cparams(("parallel", "parallel")),
        name="ffn_up",
    )(h, h, h, w_up, w_up, conv_w, conv_w, conv_b, conv_b)


def _ple_kernel(u_ref, wg_ref, p_ref, wp_ref, o_ref):
    gate = _sigmoid(jnp.dot(u_ref[...], wg_ref[...], preferred_element_type=F32))
    proj = jnp.dot(p_ref[...].astype(BF16), wp_ref[...], preferred_element_type=F32)
    o_ref[...] = proj * gate


def _ple(u, w_gate, p_all, layer, w_proj, tm=1024, tn=1024):
    m, k = u.shape
    n = w_gate.shape[1]
    p_off = layer * (m // tm)
    return pl.pallas_call(
        _ple_kernel,
        grid=(n // tn, m // tm),
        in_specs=[pl.BlockSpec((tm, k), lambda j, i: (i, 0)), pl.BlockSpec((k, tn), lambda j, i: (0, j)),
                  pl.BlockSpec((tm, PLE_DIM), lambda j, i: (i + p_off, 0)),
                  pl.BlockSpec((PLE_DIM, tn), lambda j, i: (0, j))],
        out_specs=pl.BlockSpec((tm, tn), lambda j, i: (i, j)),
        out_shape=jax.ShapeDtypeStruct((m, n), F32),
        compiler_params=_cparams(("parallel", "parallel")),
        name="ple_gate",
    )(u, w_gate, p_all, w_proj)


def _gdn_gates_kernel(x_ref, alog_ref, dtb_ref, o_ref):
    x = x_ref[...]
    n = x.shape[0]
    slot = lax.broadcasted_iota(jnp.int32, x.shape, 1) % GATE_SLOTS
    decay = -jnp.exp(alog_ref[...]) * _softplus(x + dtb_ref[...])
    ri = lax.broadcasted_iota(jnp.int32, (n, n), 0)
    ci = lax.broadcasted_iota(jnp.int32, (n, n), 1)
    prefix = _dot_01(ci <= ri, decay)
    suffix = _dot_01(ci >= ri, decay)
    vals = jnp.where(slot < 2, _sigmoid(x), jnp.where(slot == 2, prefix, suffix))
    o_ref[...] = vals.T


def _gdn_gates(proj, alog_lanes, dtb_lanes):
    m = proj.shape[0]
    cb = OFF_GATES // LANES
    return pl.pallas_call(
        _gdn_gates_kernel,
        grid=(m // GDN_CHUNK,),
        in_specs=[pl.BlockSpec((GDN_CHUNK, LANES), lambda c: (c, cb)),
                  pl.BlockSpec((1, LANES), lambda c: (0, 0)),
                  pl.BlockSpec((1, LANES), lambda c: (0, 0))],
        out_specs=pl.BlockSpec((LANES, GDN_CHUNK), lambda c: (0, c)),
        out_shape=jax.ShapeDtypeStruct((LANES, m), F32),
        compiler_params=_cparams(("parallel",)),
        name="gdn_gates",
    )(proj, alog_lanes, dtb_lanes)


GDN_INV_SPLIT_STAGES = 0
GDN_INV_REFINE = True


def _inv_dot(a, b, stage):
    return _dot3(a, b) if stage < GDN_INV_SPLIT_STAGES else _dot(a, b)


def _gdn_chains(chains):
    c = chains[0][0].shape[0]
    ri = lax.broadcasted_iota(jnp.int32, (c, c), 0)
    ci = lax.broadcasted_iota(jnp.int32, (c, c), 1)
    eye = (ri == ci).astype(F32)
    incl = {False: ci <= ri, True: ci >= ri}
    strict = {False: ci < ri, True: ci > ri}
    n = len(chains)
    rng = range(n)
    qs = [ch[0] for ch in chains]
    ks = [ch[1] for ch in chains]
    vs = [ch[2] for ch in chains]
    ss = [ch[5] for ch in chains]
    revs = [ch[6] for ch in chains]
    g_row = [jnp.broadcast_to(ch[4], (c, c)) for ch in chains]
    g_col = [g.T for g in g_row]
    beta_col = [jnp.broadcast_to(ch[3], (c, c)).T for ch in chains]
    gamma = [jnp.where(incl[revs[i]], jnp.exp(jnp.where(incl[revs[i]], g_col[i] - g_row[i], 0.0)), 0.0)
             for i in rng]
    e_col = [jnp.exp(g) for g in g_col]
    kb = [ks[i] * beta_col[i] for i in rng]
    sc = [_dot_nt(jnp.concatenate([kb[i], qs[i]], axis=0), ks[i]) for i in rng]
    a = [jnp.where(strict[revs[i]], sc[i][:c] * gamma[i], 0.0) for i in rng]
    qk = [jnp.where(incl[revs[i]], sc[i][c:] * gamma[i], 0.0) for i in rng]
    p = [-x for x in a]
    t = [eye + x for x in p]
    p = [_inv_dot(x, x, 0) for x in p]
    n_stacked = int(np.log2(c)) - 2
    for stage in range(1, n_stacked + 1):
        pt = [_inv_dot(jnp.concatenate([p[i], t[i]], axis=0), p[i], stage) for i in rng]
        t = [t[i] + pt[i][c:] for i in rng]
        p = [pt[i][:c] for i in rng]
    t = [t[i] + _inv_dot(t[i], p[i], n_stacked + 1) for i in rng]
    if GDN_INV_REFINE:
        res = [eye - t[i] - _dot3(a[i], t[i]) for i in rng]
        t = [t[i] + _dot(t[i], res[i]) for i in rng]
    uw = [_dot(t[i], jnp.concatenate([vs[i] * beta_col[i], kb[i] * e_col[i]], axis=1)) for i in rng]
    d = vs[0].shape[1]
    ws = [_dot(jnp.concatenate([uw[i][:, d:], qs[i] * e_col[i]], axis=0), ss[i]) for i in rng]
    v_new = [uw[i][:, :d] - ws[i][:c] for i in rng]
    g_last = [ch[4][:, 0:1] if ch[6] else ch[4][:, c - 1:c] for ch in chains]
    k_dec_t = [(ks[i] * jnp.exp(g_last[i] - g_col[i])).T for i in rng]
    ov = [_dot(jnp.concatenate([qk[i], k_dec_t[i]], axis=0), v_new[i]) for i in rng]
    return [(ws[i][c:] + ov[i][:c], jnp.exp(g_last[i]) * ss[i] + ov[i][c:]) for i in rng]


def _gdn_kernel(xf_ref, xfp_ref, xfn_ref, xb_ref, xbp_ref, xbn_ref, cw_ref, gf_ref, gb_ref,
                of_ref, ob_ref, sf_ref, sb_ref, *, hp, nch):
    c = pl.program_id(1)

    @pl.when(c == 0)
    def _():
        sf_ref[...] = jnp.zeros_like(sf_ref)
        sb_ref[...] = jnp.zeros_like(sb_ref)

    cw = cw_ref[...]
    dirs = ((xf_ref, xfp_ref, xfn_ref, gf_ref, of_ref, sf_ref, False, c),
            (xb_ref, xbp_ref, xbn_ref, gb_ref, ob_ref, sb_ref, True, nch - 1 - c))
    chains, sinks = [], []
    for x_ref, xp_ref, xn_ref, g_ref, o_ref, s_ref, rev, chunk in dirs:
        prev_row = xp_ref[7:8, :] * (chunk > 0).astype(F32)
        next_row = xn_ref[0:1, :] * (chunk < nch - 1).astype(F32)
        y = _centred_conv3(x_ref[...], prev_row, next_row, cw)
        y = y * _sigmoid(y)
        for h in range(hp):
            b0 = h * GDN_HEAD_COLS
            q = y[:, b0:b0 + GDN_DK]
            k = y[:, b0 + GDN_DK:b0 + 2 * GDN_DK]
            v = y[:, b0 + 2 * GDN_DK:b0 + GDN_HEAD_COLS]
            q = q * lax.rsqrt(jnp.sum(q * q, axis=-1, keepdims=True) + EPS) * np.float32(GDN_DK ** -0.5)
            k = k * lax.rsqrt(jnp.sum(k * k, axis=-1, keepdims=True) + EPS)
            g0 = h * GATE_SLOTS
            beta_row = g_ref[g0 + 1:g0 + 2, :] if rev else g_ref[g0:g0 + 1, :]
            gc_row = g_ref[g0 + 3:g0 + 4, :] if rev else g_ref[g0 + 2:g0 + 3, :]
            chains.append((q, k, v, beta_row, gc_row, s_ref[h], rev))
            sinks.append((o_ref, s_ref, h))
    for (o, s_new), (o_ref, s_ref, h) in zip(_gdn_chains(chains), sinks):
        s_ref[h] = s_new
        o_ref[:, h * GDN_DV:(h + 1) * GDN_DV] = o


def _gdn(proj, conv_w_perm, gates_t, hp=8):
    m = proj.shape[0]
    nch = m // GDN_CHUNK
    width = hp * GDN_HEAD_COLS
    cb0 = OFF_GDN // width
    r8 = GDN_CHUNK // 8
    n8 = m // 8
    fwd = lambda g, c: c
    bwd = lambda g, c: nch - 1 - c

    def main(ch):
        return pl.BlockSpec((GDN_CHUNK, width), lambda g, c: (ch(g, c), cb0 + g))

    def prev8(ch):
        return pl.BlockSpec((8, width), lambda g, c: (jnp.maximum(ch(g, c) * r8 - 1, 0), cb0 + g))

    def next8(ch):
        return pl.BlockSpec((8, width), lambda g, c: (jnp.minimum((ch(g, c) + 1) * r8, n8 - 1), cb0 + g))

    def gates(ch):
        return pl.BlockSpec((hp * GATE_SLOTS, GDN_CHUNK), lambda g, c: (g, ch(g, c)))

    def out(ch):
        return pl.BlockSpec((GDN_CHUNK, hp * GDN_DV), lambda g, c: (ch(g, c), g))

    return pl.pallas_call(
        functools.partial(_gdn_kernel, hp=hp, nch=nch),
        grid=(GDN_HEADS // hp, nch),
        in_specs=[main(fwd), prev8(fwd), next8(fwd), main(bwd), prev8(bwd), next8(bwd),
                  pl.BlockSpec((3, width), lambda g, c: (0, g)), gates(fwd), gates(bwd)],
        out_specs=[out(fwd), out(bwd)],
        out_shape=[jax.ShapeDtypeStruct((m, GDN_WIDTH), F32)] * 2,
        scratch_shapes=[pltpu.VMEM((hp, GDN_DK, GDN_DV), F32)] * 2,
        compiler_params=_cparams(("parallel", "arbitrary")),
        name="gdn",
    )(proj, proj, proj, proj, proj, proj, conv_w_perm, gates_t, gates_t)


def _gla_dir(x, lr, w2, b2, st_ref, o_ref, rev):
    rows = x.shape[0]
    nsub = rows // GLA_SUB
    logit = _dot3(lr, w2) + b2
    g = (jnp.minimum(logit, 0.0) - jnp.log(1.0 + jnp.exp(-jnp.abs(logit)))) * np.float32(1.0 / GLA_GATE_NORM)
    q = x[:, 0:GLA_DK] * np.float32(GLA_DK ** -0.5)
    k = x[:, GLA_DK:2 * GLA_DK]
    v = x[:, 2 * GLA_DK:GLA_HEAD_COLS]

    ri = lax.broadcasted_iota(jnp.int32, (rows, rows), 0)
    ci = lax.broadcasted_iota(jnp.int32, (rows, rows), 1)
    b = _dot_01((ci >= ri) if rev else (ci <= ri), g)
    b_tot = b[0:1, :] if rev else b[rows - 1:rows, :]
    st = st_ref[...]

    o = _dot_nt(q * jnp.exp(b), st)

    rid = lax.broadcasted_iota(jnp.int32, k.shape, 0)
    blocks = []
    for s in range(nsub):
        r0 = s * GLA_SUB
        if (s == nsub - 1) if rev else (s == 0):
            blocks.append(jnp.zeros((GLA_SUB, rows), F32))
            continue
        ref_row = b[r0 + GLA_SUB - 1:r0 + GLA_SUB, :] if rev else b[r0:r0 + 1, :]
        q_t = q[r0:r0 + GLA_SUB, :] * jnp.exp(b[r0:r0 + GLA_SUB, :] - ref_row)
        earlier = (rid >= r0 + GLA_SUB) if rev else (rid < r0)
        k_t = jnp.where(earlier, k * jnp.exp(jnp.minimum(ref_row - b, 0.0)), 0.0)
        blocks.append(_dot_nt(q_t, k_t))
    o = o + _dot(jnp.concatenate(blocks, axis=0), v)

    b3 = b.reshape(nsub, GLA_SUB, GLA_DK)
    q3 = q.reshape(nsub, GLA_SUB, GLA_DK)
    k3 = k.reshape(nsub, GLA_SUB, GLA_DK)
    tiles = tuple((lo, lo + 8) for lo in range(0, GLA_SUB, 8))
    lane = lax.broadcasted_iota(jnp.int32, (nsub, 8, GLA_SUB), 2)
    att_t = [jnp.zeros((nsub, 8, GLA_SUB), F32) for _ in tiles]
    for j in range(GLA_SUB):
        for ti, (lo, hi) in enumerate(tiles):
            if (lo > j) if rev else (hi - 1 < j):
                continue
            diff = b3[:, lo:hi, :] - b3[:, j:j + 1, :]
            if lo <= j < hi:
                diff = jnp.minimum(diff, 0.0)
            e = jnp.exp(diff)
            col = jnp.sum(q3[:, lo:hi, :] * e * k3[:, j:j + 1, :], axis=-1, keepdims=True)
            att_t[ti] = jnp.where(lane == j, col, att_t[ti])
    att = jnp.concatenate(att_t, axis=1)
    lane = lax.broadcasted_iota(jnp.int32, att.shape, 2)
    sub = lax.broadcasted_iota(jnp.int32, att.shape, 1)
    att = jnp.where((lane >= sub) if rev else (lane <= sub), att, 0.0)

    for s in range(nsub):
        r0 = s * GLA_SUB
        o_ref[r0:r0 + GLA_SUB, :] = o[r0:r0 + GLA_SUB, :] + _dot(att[s], v[r0:r0 + GLA_SUB, :])

    st_ref[...] = st * jnp.exp(b_tot) + _dot(v.T, k * jnp.exp(b_tot - b))


def _gla_kernel(xf_ref, xb_ref, lrf_ref, lrb_ref, w2f_ref, w2b_ref, b2f_ref, b2b_ref,
                of_ref, ob_ref, sf_ref, sb_ref):
    c = pl.program_id(1)

    @pl.when(c == 0)
    def _():
        sf_ref[...] = jnp.zeros_like(sf_ref)
        sb_ref[...] = jnp.zeros_like(sb_ref)

    _gla_dir(xf_ref[...], lrf_ref[...], w2f_ref[...], b2f_ref[...], sf_ref, of_ref, False)
    _gla_dir(xb_ref[...], lrb_ref[...], w2b_ref[...], b2b_ref[...], sb_ref, ob_ref, True)


def _gla(proj, w2f_pad, w2b_pad, b2f, b2b):
    m = proj.shape[0]
    nch = m // GLA_ROWS
    lr_cb = OFF_LR // LANES
    fwd = lambda h, c: c
    bwd = lambda h, c: nch - 1 - c

    def spec(cols, ch, col_block):
        return pl.BlockSpec((GLA_ROWS, cols), lambda h, c: (ch(h, c), col_block(h)))

    return pl.pallas_call(
        _gla_kernel,
        grid=(GLA_HEADS, nch),
        in_specs=[spec(GLA_HEAD_COLS, fwd, lambda h: h), spec(GLA_HEAD_COLS, bwd, lambda h: h),
                  spec(LANES, fwd, lambda h: lr_cb), spec(LANES, bwd, lambda h: lr_cb),
                  pl.BlockSpec((LANES, GLA_DK), lambda h, c: (0, h)),
                  pl.BlockSpec((LANES, GLA_DK), lambda h, c: (0, h)),
                  pl.BlockSpec((1, GLA_DK), lambda h, c: (0, h)),
                  pl.BlockSpec((1, GLA_DK), lambda h, c: (0, h))],
        out_specs=[spec(GLA_DV, fwd, lambda h: h), spec(GLA_DV, bwd, lambda h: h)],
        out_shape=[jax.ShapeDtypeStruct((m, GLA_WIDTH), F32)] * 2,
        scratch_shapes=[pltpu.VMEM((GLA_DV, GLA_DK), F32)] * 2,
        compiler_params=_cparams(("parallel", "arbitrary")),
        name="gla",
    )(proj, proj, proj, proj, w2f_pad, w2b_pad, b2f, b2b)


def _mix_out_kernel(af_ref, ab_ref, ag_ref, bf_ref, bb_ref, bz_ref, na_ref, nb_ref, o_ref):
    def heads(f_ref, b_ref, z_ref, gain_ref, n_heads, width, base):
        for h in range(n_heads):
            sl = slice(h * width, (h + 1) * width)
            o = _rms(f_ref[:, sl] + b_ref[:, sl]) * gain_ref[...]
            z = z_ref[:, sl]
            o_ref[:, base + h * width:base + (h + 1) * width] = (o * (z * _sigmoid(z))).astype(o_ref.dtype)

    heads(af_ref, ab_ref, ag_ref, na_ref, GLA_HEADS, GLA_DV, 0)
    heads(bf_ref, bb_ref, bz_ref, nb_ref, GDN_HEADS, GDN_DV, GLA_WIDTH)


def _mix_out(oaf, oab, obf, obb, proj, gla_norm, gdn_norm, tm=256):
    m = proj.shape[0]
    half = pl.BlockSpec((tm, GLA_WIDTH), lambda i: (i, 0))
    return pl.pallas_call(
        _mix_out_kernel,
        grid=(m // tm,),
        in_specs=[half, half, pl.BlockSpec((tm, GLA_WIDTH), lambda i: (i, OFF_AG // GLA_WIDTH)),
                  half, half, pl.BlockSpec((tm, GDN_WIDTH), lambda i: (i, OFF_BZ // GDN_WIDTH)),
                  pl.BlockSpec((1, GLA_DV), lambda i: (0, 0)), pl.BlockSpec((1, GDN_DV), lambda i: (0, 0))],
        out_specs=pl.BlockSpec((tm, D_MODEL), lambda i: (i, 0)),
        out_shape=jax.ShapeDtypeStruct((m, D_MODEL), BF16),
        compiler_params=_cparams(("parallel",)),
        name="mix_out",
    )(oaf, oab, proj, obf, obb, proj, gla_norm.reshape(1, GLA_DV), gdn_norm.reshape(1, GDN_DV))


def _permute_w_in(w):
    kdim = w.shape[0]
    idx = np.cumsum(IN_SPLITS)[:-1].tolist()
    (a_q, a_k, a_v, a_g, lr_f, lr_b, b_qkv, b_z, beta_f, beta_b, dec_f, dec_b) = jnp.split(w, idx, axis=1)
    gla = jnp.concatenate([a_q.reshape(kdim, GLA_HEADS, GLA_DK), a_k.reshape(kdim, GLA_HEADS, GLA_DK),
                           a_v.reshape(kdim, GLA_HEADS, GLA_DV)], axis=2).reshape(kdim, -1)
    d_q, d_k, d_v = jnp.split(b_qkv, [GDN_KEY, 2 * GDN_KEY], axis=1)
    gdn = jnp.stack([d_q.reshape(kdim, GDN_HEADS, GDN_DK), d_k.reshape(kdim, GDN_HEADS, GDN_DK),
                     d_v.reshape(kdim, GDN_HEADS, GDN_DV)], axis=2).reshape(kdim, -1)
    gates = jnp.stack([beta_f, beta_b, dec_f, dec_b], axis=2)
    gates = jnp.pad(gates, ((0, 0), (0, 0), (0, GATE_SLOTS - 4))).reshape(kdim, GDN_HEADS * GATE_SLOTS)
    lr = jnp.pad(jnp.concatenate([lr_f, lr_b], axis=1), ((0, 0), (0, LANES - 2 * GLA_RANK)))
    return jnp.concatenate([gla, a_g, gdn, b_z, gates, lr], axis=1).astype(BF16)


def _permute_gdn_conv(cw):
    d_q, d_k, d_v = jnp.split(cw, [GDN_KEY, 2 * GDN_KEY], axis=1)
    return jnp.stack([d_q.reshape(3, GDN_HEADS, GDN_DK), d_k.reshape(3, GDN_HEADS, GDN_DK),
                      d_v.reshape(3, GDN_HEADS, GDN_DV)], axis=2).reshape(3, -1)


def _gate_lanes(f, b):
    z = jnp.zeros_like(f)
    return jnp.stack([z, z, f, b, z, z, z, z], axis=1).reshape(1, GDN_HEADS * GATE_SLOTS)


def _pad_gate_w(w, first_row):
    return jnp.zeros((LANES, GLA_KEY), F32).at[first_row:first_row + GLA_RANK, :].set(w)


def kernel(x, p, norm_mix_pre, w_in, gla_gate_w_f, gla_gate_b_f, gla_gate_w_b, gla_gate_b_b, gla_out_norm, gdn_conv, gdn_a_log_f, gdn_dt_bias_f, gdn_a_log_b, gdn_dt_bias_b, gdn_out_norm, w_out, norm_mix_post, norm_ffn_pre, ffn_w_up, ffn_conv_w, ffn_conv_b, ffn_w_down, norm_ffn_post, ple_w_proj, ple_w_gate, norm_ple_post):
    bsz, seq, d = x.shape
    depth = w_in.shape[0]
    m = bsz * seq
    assert bsz == 1, "the scans treat the flattened rows as one sequence"
    xm = x.reshape(m, d)
    p_all = p.reshape(depth * m, PLE_DIM)
    h = _norm(xm, norm_mix_pre[0])
    for i in range(depth):
        proj = _mm(h, _permute_w_in(w_in[i]), tm=1024, tn=768, name="in_proj")
        gates_t = _gdn_gates(proj, _gate_lanes(gdn_a_log_f[i], gdn_a_log_b[i]),
                             _gate_lanes(gdn_dt_bias_f[i], gdn_dt_bias_b[i]))
        oaf, oab = _gla(proj, _pad_gate_w(gla_gate_w_f[i], 0), _pad_gate_w(gla_gate_w_b[i], GLA_RANK),
                        gla_gate_b_f[i].reshape(1, GLA_KEY), gla_gate_b_b[i].reshape(1, GLA_KEY))
        obf, obb = _gdn(proj, _permute_gdn_conv(gdn_conv[i]), gates_t)
        o = _mix_out(oaf, oab, obf, obb, proj, gla_out_norm[i], gdn_out_norm[i])
        mix = _mm(o, w_out[i].astype(BF16), tm=1024, tn=1024, name="out_proj")
        xm, h = _post(xm, mix, norm_mix_post[i], norm_ffn_pre[i])
        act = _ffn_up(h, ffn_w_up[i].astype(BF16), ffn_conv_w[i], ffn_conv_b[i])
        ffn = _mm_ksplit(act, ffn_w_down[i].astype(BF16), tm=1024, tn=1024, tk=4096, name="ffn_down")
        xm, u = _post(xm, ffn, norm_ffn_post[i], jnp.ones((d,), F32))
        t = _ple(u, ple_w_gate[i].astype(BF16), p_all, i, ple_w_proj[i].astype(BF16))
        g_next = norm_mix_pre[i + 1] if i + 1 < depth else None
        xm, h = _post(xm, t, norm_ple_post[i], g_next)
    return xm.reshape(bsz, seq, d)
```

```python
import functools

import numpy as np
import jax
import jax.numpy as jnp
from jax import lax
from jax.experimental import pallas as pl
from jax.experimental.pallas import tpu as pltpu

F32 = jnp.float32
BF16 = jnp.bfloat16
HIGHEST = lax.Precision.HIGHEST

D_MODEL = 4096
PLE_DIM = 256
GLA_HEADS, GLA_DK, GLA_DV, GLA_RANK = 4, 256, 512, 16
GLA_KEY = GLA_HEADS * GLA_DK
GLA_WIDTH = GLA_HEADS * GLA_DV
GLA_GATE_NORM = 16.0
GDN_HEADS, GDN_DK, GDN_DV = 16, 128, 128
GDN_KEY = GDN_HEADS * GDN_DK
GDN_WIDTH = GDN_HEADS * GDN_DV
D_FF = 4 * D_MODEL
EPS = 1e-6
IN_SPLITS = (GLA_KEY, GLA_KEY, GLA_WIDTH, GLA_WIDTH, GLA_RANK, GLA_RANK,
             2 * GDN_KEY + GDN_WIDTH, GDN_WIDTH, GDN_HEADS, GDN_HEADS, GDN_HEADS, GDN_HEADS)

GLA_HEAD_COLS = 2 * GLA_DK + GLA_DV
GDN_HEAD_COLS = 2 * GDN_DK + GDN_DV
OFF_GLA = 0
OFF_AG = OFF_GLA + GLA_HEADS * GLA_HEAD_COLS
OFF_GDN = OFF_AG + GLA_WIDTH
OFF_BZ = OFF_GDN + GDN_HEADS * GDN_HEAD_COLS
OFF_GATES = OFF_BZ + GDN_WIDTH
OFF_LR = OFF_GATES + 128
D_PROJ = OFF_LR + 128
GATE_SLOTS = 8

LANES = 128
GDN_CHUNK = 128
GLA_SUB = 16
GLA_ROWS = 256
VMEM_LIMIT = 56 * 1024 * 1024


def _cparams(sem):
    return pltpu.CompilerParams(dimension_semantics=sem, vmem_limit_bytes=VMEM_LIMIT)


def _dot(a, b):
    return jnp.dot(a.astype(BF16), b.astype(BF16), preferred_element_type=F32)


def _dot_nt(a, b):
    return lax.dot_general(a.astype(BF16), b.astype(BF16), (((1,), (1,)), ((), ())),
                           preferred_element_type=F32)


def _dot_tn(a, b):
    return lax.dot_general(a.astype(BF16), b.astype(BF16), (((0,), (0,)), ((), ())),
                           preferred_element_type=F32)


def _split(a):
    hi = a.astype(BF16)
    lo = (a - hi.astype(F32)).astype(BF16)
    return hi, lo


def _dot3(a, b):
    ah, al = _split(a)
    bh, bl = _split(b)
    d = functools.partial(jnp.dot, preferred_element_type=F32)
    return d(ah, bh) + (d(ah, bl) + d(al, bh))


def _dot_01(mask, b):
    b1 = b.astype(BF16)
    r1 = b - b1.astype(F32)
    b2 = r1.astype(BF16)
    b3 = (r1 - b2.astype(F32)).astype(BF16)
    d = functools.partial(jnp.dot, mask.astype(F32).astype(BF16), preferred_element_type=F32)
    return d(b1) + (d(b2) + d(b3))


def _sigmoid(x):
    return 1.0 / (1.0 + jnp.exp(-x))


def _softplus(x):
    return jnp.maximum(x, 0.0) + jnp.log1p(jnp.exp(-jnp.abs(x)))


def _rms(x):
    return x * lax.rsqrt(jnp.mean(x * x, axis=-1, keepdims=True) + EPS)


def _centred_conv3(x, prev_row, next_row, w):
    rows = x.shape[0]
    rid = lax.broadcasted_iota(jnp.int32, x.shape, 0)
    x_prev = jnp.where(rid == 0, prev_row, pltpu.roll(x, 1, 0))
    x_next = jnp.where(rid == rows - 1, next_row, pltpu.roll(x, rows - 1, 0))
    return x_prev * w[0:1, :] + x * w[1:2, :] + x_next * w[2:3, :]


def _norm_kernel(x_ref, g_ref, o_ref):
    o_ref[...] = (_rms(x_ref[...]) * g_ref[...]).astype(o_ref.dtype)


def _norm(x, gain, tm=256):
    m, d = x.shape
    return pl.pallas_call(
        _norm_kernel,
        grid=(m // tm,),
        in_specs=[pl.BlockSpec((tm, d), lambda i: (i, 0)), pl.BlockSpec((1, d), lambda i: (0, 0))],
        out_specs=pl.BlockSpec((tm, d), lambda i: (i, 0)),
        out_shape=jax.ShapeDtypeStruct((m, d), BF16),
        compiler_params=_cparams(("parallel",)),
        name="rms_norm",
    )(x, gain.reshape(1, d))


def _post_kernel(x_ref, y_ref, gp_ref, gn_ref, xo_ref, ho_ref):
    xn = x_ref[...] + _rms(y_ref[...]) * gp_ref[...]
    xo_ref[...] = xn
    ho_ref[...] = (_rms(xn) * gn_ref[...]).astype(ho_ref.dtype)


def _post_last_kernel(x_ref, y_ref, gp_ref, xo_ref):
    xo_ref[...] = x_ref[...] + _rms(y_ref[...]) * gp_ref[...]


def _post(x, y, g_post, g_next, tm=256):
    m, d = x.shape
    row = pl.BlockSpec((tm, d), lambda i: (i, 0))
    vec = pl.BlockSpec((1, d), lambda i: (0, 0))
    if g_next is None:
        return pl.pallas_call(
            _post_last_kernel, grid=(m // tm,), in_specs=[row, row, vec], out_specs=row,
            out_shape=jax.ShapeDtypeStruct((m, d), F32), compiler_params=_cparams(("parallel",)),
            name="post_last")(x, y, g_post.reshape(1, d)), None
    return pl.pallas_call(
        _post_kernel, grid=(m // tm,), in_specs=[row, row, vec, vec], out_specs=[row, row],
        out_shape=[jax.ShapeDtypeStruct((m, d), F32), jax.ShapeDtypeStruct((m, d), BF16)],
        compiler_params=_cparams(("parallel",)), name="post",
    )(x, y, g_post.reshape(1, d), g_next.reshape(1, d))


def _mm_kernel(a_ref, w_ref, o_ref):
    o_ref[...] = jnp.dot(a_ref[...], w_ref[...], preferred_element_type=F32).astype(o_ref.dtype)


def _mm(a, w, tm, tn, out_dtype=F32, name="mm"):
    m, k = a.shape
    n = w.shape[1]
    return pl.pallas_call(
        _mm_kernel,
        grid=(n // tn, m // tm),
        in_specs=[pl.BlockSpec((tm, k), lambda j, i: (i, 0)), pl.BlockSpec((k, tn), lambda j, i: (0, j))],
        out_specs=pl.BlockSpec((tm, tn), lambda j, i: (i, j)),
        out_shape=jax.ShapeDtypeStruct((m, n), out_dtype),
        compiler_params=_cparams(("parallel", "parallel")),
        name=name,
    )(a, w)


def _mm_ksplit_kernel(a_ref, w_ref, o_ref, acc_ref, *, nk):
    kk = pl.program_id(2)

    @pl.when(kk == 0)
    def _():
        acc_ref[...] = jnp.zeros_like(acc_ref)

    acc_ref[...] += jnp.dot(a_ref[...], w_ref[...], preferred_element_type=F32)

    @pl.when(kk == nk - 1)
    def _():
        o_ref[...] = acc_ref[...]


def _mm_ksplit(a, w, tm, tn, tk, name="mm_ksplit"):
    m, k = a.shape
    n = w.shape[1]
    nk = k // tk
    return pl.pallas_call(
        functools.partial(_mm_ksplit_kernel, nk=nk),
        grid=(m // tm, n // tn, nk),
        in_specs=[pl.BlockSpec((tm, tk), lambda i, j, kk: (i, kk)),
                  pl.BlockSpec((tk, tn), lambda i, j, kk: (kk, j))],
        out_specs=pl.BlockSpec((tm, tn), lambda i, j, kk: (i, j)),
        out_shape=jax.ShapeDtypeStruct((m, n), F32),
        scratch_shapes=[pltpu.VMEM((tm, tn), F32)],
        compiler_params=_cparams(("parallel", "parallel", "arbitrary")),
        name=name,
    )(a, w)


HALO = 16


def _ffn_up_kernel(h_ref, hp_ref, hn_ref, wg_ref, wu_ref, cwg_ref, cwu_ref, cbg_ref, cbu_ref, o_ref,
                   *, tm, nm):
    i = pl.program_id(1)
    lhs = jnp.concatenate([hp_ref[...], h_ref[...], hn_ref[...]], axis=0)
    not_first = (i > 0).astype(F32)
    not_last = (i < nm - 1).astype(F32)

    def branch(w_ref, cw_ref, cb_ref):
        y = jnp.dot(lhs, w_ref[...], preferred_element_type=F32)
        prev_row = y[HALO - 1:HALO, :] * not_first
        next_row = y[HALO + tm:HALO + tm + 1, :] * not_last
        return _centred_conv3(y[HALO:HALO + tm, :], prev_row, next_row, cw_ref[...]) + cb_ref[...]

    gate = branch(wg_ref, cwg_ref, cbg_ref)
    up = branch(wu_ref, cwu_ref, cbu_ref)
    c0 = np.float32(np.sqrt(2.0 / np.pi))
    gelu = 0.5 * gate * (1.0 + jnp.tanh(c0 * (gate + np.float32(0.044715) * (gate * gate * gate))))
    o_ref[...] = (gelu * up).astype(o_ref.dtype)


def _ffn_up(h, w_up, conv_w, conv_b, tm=1024, tn=512):
    m, k = h.shape
    nm, nn = m // tm, D_FF // tn
    rb = tm // HALO
    nhb = m // HALO
    conv_b = conv_b.reshape(1, 2 * D_FF)
    return pl.pallas_call(
        functools.partial(_ffn_up_kernel, tm=tm, nm=nm),
        grid=(nn, nm),
        in_specs=[
            pl.BlockSpec((tm, k), lambda j, i: (i, 0)),
            pl.BlockSpec((HALO, k), lambda j, i: (jnp.maximum(i * rb - 1, 0), 0)),
            pl.BlockSpec((HALO, k), lambda j, i: (jnp.minimum((i + 1) * rb, nhb - 1), 0)),
            pl.BlockSpec((k, tn), lambda j, i: (0, j)),
            pl.BlockSpec((k, tn), lambda j, i: (0, j + nn)),
            pl.BlockSpec((3, tn), lambda j, i: (0, j)),
            pl.BlockSpec((3, tn), lambda j, i: (0, j + nn)),
            pl.BlockSpec((1, tn), lambda j, i: (0, j)),
            pl.BlockSpec((1, tn), lambda j, i: (0, j + nn)),
        ],
        out_specs=pl.BlockSpec((tm, tn), lambda j, i: (i, j)),
        out_shape=jax.ShapeDtypeStruct((m, D_FF), BF16),
        compiler_params=_cparams(("parallel", "parallel")),
        name="ffn_up",
    )(h, h, h, w_up, w_up, conv_w, conv_w, conv_b, conv_b)


def _ple_kernel(u_ref, wg_ref, p_ref, wp_ref, o_ref):
    gate = _sigmoid(jnp.dot(u_ref[...], wg_ref[...], preferred_element_type=F32))
    proj = jnp.dot(p_ref[...].astype(BF16), wp_ref[...], preferred_element_type=F32)
    o_ref[...] = proj * gate


def _ple(u, w_gate, p_all, layer, w_proj, tm=1024, tn=1024):
    m, k = u.shape
    n = w_gate.shape[1]
    p_off = layer * (m // tm)
    return pl.pallas_call(
        _ple_kernel,
        grid=(n // tn, m // tm),
        in_specs=[pl.BlockSpec((tm, k), lambda j, i: (i, 0)), pl.BlockSpec((k, tn), lambda j, i: (0, j)),
                  pl.BlockSpec((tm, PLE_DIM), lambda j, i: (i + p_off, 0)),
                  pl.BlockSpec((PLE_DIM, tn), lambda j, i: (0, j))],
        out_specs=pl.BlockSpec((tm, tn), lambda j, i: (i, j)),
        out_shape=jax.ShapeDtypeStruct((m, n), F32),
        compiler_params=_cparams(("parallel", "parallel")),
        name="ple_gate",
    )(u, w_gate, p_all, w_proj)


def _gdn_gates_kernel(x_ref, alog_ref, dtb_ref, o_ref):
    x = x_ref[...]
    n = x.shape[0]
    slot = lax.broadcasted_iota(jnp.int32, x.shape, 1) % GATE_SLOTS
    decay = -jnp.exp(alog_ref[...]) * _softplus(x + dtb_ref[...])
    ri = lax.broadcasted_iota(jnp.int32, (n, n), 0)
    ci = lax.broadcasted_iota(jnp.int32, (n, n), 1)
    prefix = _dot_01(ci <= ri, decay)
    suffix = _dot_01(ci >= ri, decay)
    vals = jnp.where(slot < 2, _sigmoid(x), jnp.where(slot == 2, prefix, suffix))
    o_ref[...] = vals.T


def _gdn_gates(proj, alog_lanes, dtb_lanes):
    m = proj.shape[0]
    cb = OFF_GATES // LANES
    return pl.pallas_call(
        _gdn_gates_kernel,
        grid=(m // GDN_CHUNK,),
        in_specs=[pl.BlockSpec((GDN_CHUNK, LANES), lambda c: (c, cb)),
                  pl.BlockSpec((1, LANES), lambda c: (0, 0)),
                  pl.BlockSpec((1, LANES), lambda c: (0, 0))],
        out_specs=pl.BlockSpec((LANES, GDN_CHUNK), lambda c: (0, c)),
        out_shape=jax.ShapeDtypeStruct((LANES, m), F32),
        compiler_params=_cparams(("parallel",)),
        name="gdn_gates",
    )(proj, alog_lanes, dtb_lanes)


GDN_INV_SPLIT_STAGES = 0
GDN_INV_REFINE = True


def _inv_dot(a, b, stage):
    return _dot3(a, b) if stage < GDN_INV_SPLIT_STAGES else _dot(a, b)


def _gdn_chains(chains):
    c = chains[0][0].shape[0]
    ri = lax.broadcasted_iota(jnp.int32, (c, c), 0)
    ci = lax.broadcasted_iota(jnp.int32, (c, c), 1)
    eye = (ri == ci).astype(F32)
    incl = {False: ci <= ri, True: ci >= ri}
    strict = {False: ci < ri, True: ci > ri}
    n = len(chains)
    rng = range(n)
    qs = [ch[0] for ch in chains]
    ks = [ch[1] for ch in chains]
    vs = [ch[2] for ch in chains]
    ss = [ch[5] for ch in chains]
    revs = [ch[6] for ch in chains]
    g_row = [jnp.broadcast_to(ch[4], (c, c)) for ch in chains]
    g_col = [g.T for g in g_row]
    beta_col = [jnp.broadcast_to(ch[3], (c, c)).T for ch in chains]
    gamma = [jnp.where(incl[revs[i]], jnp.exp(jnp.where(incl[revs[i]], g_col[i] - g_row[i], 0.0)), 0.0)
             for i in rng]
    e_col = [jnp.exp(g) for g in g_col]
    kb = [ks[i] * beta_col[i] for i in rng]
    sc = [_dot_nt(jnp.concatenate([kb[i], qs[i]], axis=0), ks[i]) for i in rng]
    a = [jnp.where(strict[revs[i]], sc[i][:c] * gamma[i], 0.0) for i in rng]
    qk = [jnp.where(incl[revs[i]], sc[i][c:] * gamma[i], 0.0) for i in rng]
    p = [-x for x in a]
    t = [eye + x for x in p]
    p = [_inv_dot(x, x, 0) for x in p]
    n_stacked = int(np.log2(c)) - 2
    for stage in range(1, n_stacked + 1):
        pt = [_inv_dot(jnp.concatenate([p[i], t[i]], axis=0), p[i], stage) for i in rng]
        t = [t[i] + pt[i][c:] for i in rng]
        p = [pt[i][:c] for i in rng]
    t = [t[i] + _inv_dot(t[i], p[i], n_stacked + 1) for i in rng]
    if GDN_INV_REFINE:
        res = [eye - t[i] - _dot3(a[i], t[i]) for i in rng]
        t = [t[i] + _dot(t[i], res[i]) for i in rng]
    uw = [_dot(t[i], jnp.concatenate([vs[i] * beta_col[i], kb[i] * e_col[i]], axis=1)) for i in rng]
    d = vs[0].shape[1]
    ws = [_dot(jnp.concatenate([uw[i][:, d:], qs[i] * e_col[i]], axis=0), ss[i]) for i in rng]
    v_new = [uw[i][:, :d] - ws[i][:c] for i in rng]
    g_last = [ch[4][:, 0:1] if ch[6] else ch[4][:, c - 1:c] for ch in chains]
    k_dec_t = [(ks[i] * jnp.exp(g_last[i] - g_col[i])).T for i in rng]
    ov = [_dot(jnp.concatenate([qk[i], k_dec_t[i]], axis=0), v_new[i]) for i in rng]
    return [(ws[i][c:] + ov[i][:c], jnp.exp(g_last[i]) * ss[i] + ov[i][c:]) for i in rng]


def _gdn_kernel(xf_ref, xfp_ref, xfn_ref, xb_ref, xbp_ref, xbn_ref, cw_ref, gf_ref, gb_ref,
                of_ref, ob_ref, sf_ref, sb_ref, *, hp, nch):
    c = pl.program_id(1)

    @pl.when(c == 0)
    def _():
        sf_ref[...] = jnp.zeros_like(sf_ref)
        sb_ref[...] = jnp.zeros_like(sb_ref)

    cw = cw_ref[...]
    dirs = ((xf_ref, xfp_ref, xfn_ref, gf_ref, of_ref, sf_ref, False, c),
            (xb_ref, xbp_ref, xbn_ref, gb_ref, ob_ref, sb_ref, True, nch - 1 - c))
    chains, sinks = [], []
    for x_ref, xp_ref, xn_ref, g_ref, o_ref, s_ref, rev, chunk in dirs:
        prev_row = xp_ref[7:8, :] * (chunk > 0).astype(F32)
        next_row = xn_ref[0:1, :] * (chunk < nch - 1).astype(F32)
        y = _centred_conv3(x_ref[...], prev_row, next_row, cw)
        y = y * _sigmoid(y)
        for h in range(hp):
            q = y[:, h * GDN_DK:(h + 1) * GDN_DK]
            k = y[:, GDN_KEY + h * GDN_DK:GDN_KEY + (h + 1) * GDN_DK]
            v = y[:, 2 * GDN_KEY + h * GDN_DV:2 * GDN_KEY + (h + 1) * GDN_DV]
            q = q * lax.rsqrt(jnp.sum(q * q, axis=-1, keepdims=True) + EPS) * np.float32(GDN_DK ** -0.5)
            k = k * lax.rsqrt(jnp.sum(k * k, axis=-1, keepdims=True) + EPS)
            g0 = h * GATE_SLOTS
            beta_row = g_ref[g0 + 1:g0 + 2, :] if rev else g_ref[g0:g0 + 1, :]
            gc_row = g_ref[g0 + 3:g0 + 4, :] if rev else g_ref[g0 + 2:g0 + 3, :]
            chains.append((q, k, v, beta_row, gc_row, s_ref[h], rev))
            sinks.append((o_ref, s_ref, h))
    for (o, s_new), (o_ref, s_ref, h) in zip(_gdn_chains(chains), sinks):
        s_ref[h] = s_new
        o_ref[:, h * GDN_DV:(h + 1) * GDN_DV] = o


def _gdn(proj, conv_w, gates_t):
    hp = GDN_HEADS
    m = proj.shape[0]
    nch = m // GDN_CHUNK
    width = hp * GDN_HEAD_COLS
    cb0 = OFF_GDN // width
    r8 = GDN_CHUNK // 8
    n8 = m // 8
    fwd = lambda g, c: c
    bwd = lambda g, c: nch - 1 - c

    def main(ch):
        return pl.BlockSpec((GDN_CHUNK, width), lambda g, c: (ch(g, c), cb0 + g))

    def prev8(ch):
        return pl.BlockSpec((8, width), lambda g, c: (jnp.maximum(ch(g, c) * r8 - 1, 0), cb0 + g))

    def next8(ch):
        return pl.BlockSpec((8, width), lambda g, c: (jnp.minimum((ch(g, c) + 1) * r8, n8 - 1), cb0 + g))

    def gates(ch):
        return pl.BlockSpec((hp * GATE_SLOTS, GDN_CHUNK), lambda g, c: (g, ch(g, c)))

    def out(ch):
        return pl.BlockSpec((GDN_CHUNK, hp * GDN_DV), lambda g, c: (ch(g, c), g))

    return pl.pallas_call(
        functools.partial(_gdn_kernel, hp=hp, nch=nch),
        grid=(GDN_HEADS // hp, nch),
        in_specs=[main(fwd), prev8(fwd), next8(fwd), main(bwd), prev8(bwd), next8(bwd),
                  pl.BlockSpec((3, width), lambda g, c: (0, g)), gates(fwd), gates(bwd)],
        out_specs=[out(fwd), out(bwd)],
        out_shape=[jax.ShapeDtypeStruct((m, GDN_WIDTH), F32)] * 2,
        scratch_shapes=[pltpu.VMEM((hp, GDN_DK, GDN_DV), F32)] * 2,
        compiler_params=_cparams(("parallel", "arbitrary")),
        name="gdn",
    )(proj, proj, proj, proj, proj, proj, conv_w, gates_t, gates_t)


def _gla_dir(q, k, v, lr, w2, b2, st_ref, o_ref, rev):
    rows = q.shape[0]
    nsub = rows // GLA_SUB
    logit = _dot3(lr, w2) + b2
    g = (jnp.minimum(logit, 0.0) - jnp.log(1.0 + jnp.exp(-jnp.abs(logit)))) * np.float32(1.0 / GLA_GATE_NORM)
    q = q * np.float32(GLA_DK ** -0.5)

    ri = lax.broadcasted_iota(jnp.int32, (rows, rows), 0)
    ci = lax.broadcasted_iota(jnp.int32, (rows, rows), 1)
    b = _dot_01((ci >= ri) if rev else (ci <= ri), g)
    b_tot = b[0:1, :] if rev else b[rows - 1:rows, :]
    st = st_ref[...]

    o = _dot_nt(q * jnp.exp(b), st)

    rid = lax.broadcasted_iota(jnp.int32, k.shape, 0)
    blocks = []
    for s in range(nsub):
        r0 = s * GLA_SUB
        if (s == nsub - 1) if rev else (s == 0):
            blocks.append(jnp.zeros((GLA_SUB, rows), F32))
            continue
        ref_row = b[r0 + GLA_SUB - 1:r0 + GLA_SUB, :] if rev else b[r0:r0 + 1, :]
        q_t = q[r0:r0 + GLA_SUB, :] * jnp.exp(b[r0:r0 + GLA_SUB, :] - ref_row)
        earlier = (rid >= r0 + GLA_SUB) if rev else (rid < r0)
        k_t = jnp.where(earlier, k * jnp.exp(jnp.minimum(ref_row - b, 0.0)), 0.0)
        blocks.append(_dot_nt(q_t, k_t))
    o = o + _dot(jnp.concatenate(blocks, axis=0), v)

    b3 = b.reshape(nsub, GLA_SUB, GLA_DK)
    q3 = q.reshape(nsub, GLA_SUB, GLA_DK)
    k3 = k.reshape(nsub, GLA_SUB, GLA_DK)
    tiles = tuple((lo, lo + 8) for lo in range(0, GLA_SUB, 8))
    lane = lax.broadcasted_iota(jnp.int32, (nsub, 8, GLA_SUB), 2)
    att_t = [jnp.zeros((nsub, 8, GLA_SUB), F32) for _ in tiles]
    for j in range(GLA_SUB):
        for ti, (lo, hi) in enumerate(tiles):
            if (lo > j) if rev else (hi - 1 < j):
                continue
            diff = b3[:, lo:hi, :] - b3[:, j:j + 1, :]
            if lo <= j < hi:
                diff = jnp.minimum(diff, 0.0)
            e = jnp.exp(diff)
            col = jnp.sum(q3[:, lo:hi, :] * e * k3[:, j:j + 1, :], axis=-1, keepdims=True)
            att_t[ti] = jnp.where(lane == j, col, att_t[ti])
    att = jnp.concatenate(att_t, axis=1)
    lane = lax.broadcasted_iota(jnp.int32, att.shape, 2)
    sub = lax.broadcasted_iota(jnp.int32, att.shape, 1)
    att = jnp.where((lane >= sub) if rev else (lane <= sub), att, 0.0)

    for s in range(nsub):
        r0 = s * GLA_SUB
        o_ref[r0:r0 + GLA_SUB, :] = o[r0:r0 + GLA_SUB, :] + _dot(att[s], v[r0:r0 + GLA_SUB, :])

    st_ref[...] = st * jnp.exp(b_tot) + _dot(v.T, k * jnp.exp(b_tot - b))


def _gla_kernel(qf_ref, kf_ref, vf_ref, qb_ref, kb_ref, vb_ref, lrf_ref, lrb_ref, w2f_ref, w2b_ref,
                b2f_ref, b2b_ref, of_ref, ob_ref, sf_ref, sb_ref):
    c = pl.program_id(1)

    @pl.when(c == 0)
    def _():
        sf_ref[...] = jnp.zeros_like(sf_ref)
        sb_ref[...] = jnp.zeros_like(sb_ref)

    _gla_dir(qf_ref[...], kf_ref[...], vf_ref[...], lrf_ref[...], w2f_ref[...], b2f_ref[...],
             sf_ref, of_ref, False)
    _gla_dir(qb_ref[...], kb_ref[...], vb_ref[...], lrb_ref[...], w2b_ref[...], b2b_ref[...],
             sb_ref, ob_ref, True)


def _gla(proj, w2f_pad, w2b_pad, b2f, b2b):
    m = proj.shape[0]
    nch = m // GLA_ROWS
    lr_cb = OFF_LR // LANES
    fwd = lambda h, c: c
    bwd = lambda h, c: nch - 1 - c

    def spec(cols, ch, col_block):
        return pl.BlockSpec((GLA_ROWS, cols), lambda h, c: (ch(h, c), col_block(h)))

    def qkv(ch):
        return (spec(GLA_DK, ch, lambda h: OFF_GLA // GLA_DK + h),
                spec(GLA_DK, ch, lambda h: (OFF_GLA + GLA_KEY) // GLA_DK + h),
                spec(GLA_DV, ch, lambda h: (OFF_GLA + 2 * GLA_KEY) // GLA_DV + h))

    return pl.pallas_call(
        _gla_kernel,
        grid=(GLA_HEADS, nch),
        in_specs=[*qkv(fwd), *qkv(bwd),
                  spec(LANES, fwd, lambda h: lr_cb), spec(LANES, bwd, lambda h: lr_cb),
                  pl.BlockSpec((LANES, GLA_DK), lambda h, c: (0, h)),
                  pl.BlockSpec((LANES, GLA_DK), lambda h, c: (0, h)),
                  pl.BlockSpec((1, GLA_DK), lambda h, c: (0, h)),
                  pl.BlockSpec((1, GLA_DK), lambda h, c: (0, h))],
        out_specs=[spec(GLA_DV, fwd, lambda h: h), spec(GLA_DV, bwd, lambda h: h)],
        out_shape=[jax.ShapeDtypeStruct((m, GLA_WIDTH), F32)] * 2,
        scratch_shapes=[pltpu.VMEM((GLA_DV, GLA_DK), F32)] * 2,
        compiler_params=_cparams(("parallel", "arbitrary")),
        name="gla",
    )(proj, proj, proj, proj, proj, proj, proj, proj, w2f_pad, w2b_pad, b2f, b2b)


def _mix_out_kernel(af_ref, ab_ref, ag_ref, bf_ref, bb_ref, bz_ref, na_ref, nb_ref, o_ref):
    def heads(f_ref, b_ref, z_ref, gain_ref, n_heads, width, base):
        for h in range(n_heads):
            sl = slice(h * width, (h + 1) * width)
            o = _rms(f_ref[:, sl] + b_ref[:, sl]) * gain_ref[...]
            z = z_ref[:, sl]
            o_ref[:, base + h * width:base + (h + 1) * width] = (o * (z * _sigmoid(z))).astype(o_ref.dtype)

    heads(af_ref, ab_ref, ag_ref, na_ref, GLA_HEADS, GLA_DV, 0)
    heads(bf_ref, bb_ref, bz_ref, nb_ref, GDN_HEADS, GDN_DV, GLA_WIDTH)


def _mix_out(oaf, oab, obf, obb, proj, gla_norm, gdn_norm, tm=256):
    m = proj.shape[0]
    half = pl.BlockSpec((tm, GLA_WIDTH), lambda i: (i, 0))
    return pl.pallas_call(
        _mix_out_kernel,
        grid=(m // tm,),
        in_specs=[half, half, pl.BlockSpec((tm, GLA_WIDTH), lambda i: (i, OFF_AG // GLA_WIDTH)),
                  half, half, pl.BlockSpec((tm, GDN_WIDTH), lambda i: (i, OFF_BZ // GDN_WIDTH)),
                  pl.BlockSpec((1, GLA_DV), lambda i: (0, 0)), pl.BlockSpec((1, GDN_DV), lambda i: (0, 0))],
        out_specs=pl.BlockSpec((tm, D_MODEL), lambda i: (i, 0)),
        out_shape=jax.ShapeDtypeStruct((m, D_MODEL), BF16),
        compiler_params=_cparams(("parallel",)),
        name="mix_out",
    )(oaf, oab, proj, obf, obb, proj, gla_norm.reshape(1, GLA_DV), gdn_norm.reshape(1, GDN_DV))


def _permute_w_in(w):
    kdim = w.shape[0]
    idx = np.cumsum(IN_SPLITS)[:-1].tolist()
    (a_q, a_k, a_v, a_g, lr_f, lr_b, b_qkv, b_z, beta_f, beta_b, dec_f, dec_b) = jnp.split(w, idx, axis=1)
    gates = jnp.stack([beta_f, beta_b, dec_f, dec_b], axis=2)
    gates = jnp.pad(gates, ((0, 0), (0, 0), (0, GATE_SLOTS - 4))).reshape(kdim, GDN_HEADS * GATE_SLOTS)
    lr = jnp.pad(jnp.concatenate([lr_f, lr_b], axis=1), ((0, 0), (0, LANES - 2 * GLA_RANK)))
    return jnp.concatenate([w[:, :OFF_GDN], w[:, idx[5]:idx[7]], gates, lr], axis=1).astype(BF16)


def _gate_lanes(f, b):
    z = jnp.zeros_like(f)
    return jnp.stack([z, z, f, b, z, z, z, z], axis=1).reshape(1, GDN_HEADS * GATE_SLOTS)


def _pad_gate_w(w, first_row):
    return jnp.zeros((LANES, GLA_KEY), F32).at[first_row:first_row + GLA_RANK, :].set(w)


def kernel(x, p, norm_mix_pre, w_in, gla_gate_w_f, gla_gate_b_f, gla_gate_w_b, gla_gate_b_b, gla_out_norm, gdn_conv, gdn_a_log_f, gdn_dt_bias_f, gdn_a_log_b, gdn_dt_bias_b, gdn_out_norm, w_out, norm_mix_post, norm_ffn_pre, ffn_w_up, ffn_conv_w, ffn_conv_b, ffn_w_down, norm_ffn_post, ple_w_proj, ple_w_gate, norm_ple_post):
    bsz, seq, d = x.shape
    depth = w_in.shape[0]
    m = bsz * seq
    assert bsz == 1, "the scans treat the flattened rows as one sequence"
    xm = x.reshape(m, d)
    p_all = p.reshape(depth * m, PLE_DIM)
    h = _norm(xm, norm_mix_pre[0])
    for i in range(depth):
        proj = _mm(h, _permute_w_in(w_in[i]), tm=1024, tn=768, name="in_proj")
        gates_t = _gdn_gates(proj, _gate_lanes(gdn_a_log_f[i], gdn_a_log_b[i]),
                             _gate_lanes(gdn_dt_bias_f[i], gdn_dt_bias_b[i]))
        oaf, oab = _gla(proj, _pad_gate_w(gla_gate_w_f[i], 0), _pad_gate_w(gla_gate_w_b[i], GLA_RANK),
                        gla_gate_b_f[i].reshape(1, GLA_KEY), gla_gate_b_b[i].reshape(1, GLA_KEY))
        obf, obb = _gdn(proj, gdn_conv[i], gates_t)
        o = _mix_out(oaf, oab, obf, obb, proj, gla_out_norm[i], gdn_out_norm[i])
        mix = _mm(o, w_out[i].astype(BF16), tm=1024, tn=1024, name="out_proj")
        xm, h = _post(xm, mix, norm_mix_post[i], norm_ffn_pre[i])
        act = _ffn_up(h, ffn_w_up[i].astype(BF16), ffn_conv_w[i], ffn_conv_b[i])
        ffn = _mm_ksplit(act, ffn_w_down[i].astype(BF16), tm=1024, tn=1024, tk=4096, name="ffn_down")
        xm, u = _post(xm, ffn, norm_ffn_post[i], jnp.ones((d,), F32))
        t = _ple(u, ple_w_gate[i].astype(BF16), p_all, i, ple_w_proj[i].astype(BF16))
        g_next = norm_mix_pre[i + 1] if i + 1 < depth else None
        xm, h = _post(xm, t, norm_ple_post[i], g_next)
    return xm.reshape(bsz, seq, d)
```
